```python
import jax, jax.numpy as jnp
from jax import lax
import numpy as np

D_MODEL = 1024
BATCH = 16
SEQ = 4096
DEPTH = 1
DEC_BATCH = 32
DEC_SEQ = 64
PAST_LEN = 1024

CHUNK = 64
EPS = 1e-6
MIX_WIDTH = D_MODEL
HG_WIDTH = MIX_WIDTH // 2
GLA_WIDTH = MIX_WIDTH - HG_WIDTH
HG_HEADS = 4
HG_DK = 128
HG_DV = HG_WIDTH // HG_HEADS
GLA_HEADS = 4
GLA_DV = GLA_WIDTH // GLA_HEADS
GLA_DK = GLA_DV // 2
GLA_GATE_RANK = 16
GLA_GATE_NORMALIZER = 16.0
IN_SIZES = (HG_HEADS * HG_DK, HG_HEADS * HG_DK, HG_WIDTH, HG_WIDTH,
            GLA_HEADS * GLA_DK, GLA_HEADS * GLA_DK, GLA_WIDTH, GLA_WIDTH, GLA_GATE_RANK)
D_IN = sum(IN_SIZES)
N_EXPERTS = 256
TOP_K = 8
N_GROUPS = 8
TOPK_GROUPS = 4
D_EXPERT = 256
ROUTE_SCALE = 2.5
MOE_BLOCK = 128

kernel_name = 'hymba_hgrn2_gla_moe_adaln_stream_step'


def _rmsnorm(x, gain):
    xf = x.astype(jnp.float32)
    y = xf * lax.rsqrt(jnp.mean(xf * xf, axis=-1, keepdims=True) + EPS)
    return (y * gain.astype(jnp.float32)).astype(x.dtype)


def _gated_linear_recurrence(q, k, v, log_a, s0):
    B, L, H, dk = q.shape
    dv = v.shape[-1]
    c = min(CHUNK, L)
    n = L // c

    def to_chunks(t):
        return t.reshape(B, n, c, H, t.shape[-1]).transpose(1, 0, 3, 2, 4)

    qc, kc, vc, ac = to_chunks(q), to_chunks(k), to_chunks(v), to_chunks(log_a)
    causal = jnp.tril(jnp.ones((c, c), dtype=bool))[:, :, None]

    def step(s, inp):
        qi, ki, vi, ai = inp
        b = jnp.cumsum(ai, axis=2)
        diff = b[:, :, :, None, :] - b[:, :, None, :, :]
        decay = jnp.exp(jnp.where(causal, diff, -jnp.inf))
        att = jnp.sum(qi[:, :, :, None, :] * ki[:, :, None, :, :] * decay, axis=-1)
        o = jnp.einsum('bhts,bhsv->bhtv', att, vi) + jnp.einsum('bhtk,bhkv->bhtv', qi * jnp.exp(b), s)
        b_last = b[:, :, -1:, :]
        s_new = jnp.exp(b_last[:, :, 0, :])[..., None] * s + jnp.einsum(
            'bhsk,bhsv->bhkv', ki * jnp.exp(b_last - b), vi)
        return s_new, o

    s_fin, oc = lax.scan(step, s0, (qc, kc, vc, ac))
    o = oc.transpose(1, 0, 3, 2, 4).reshape(B, L, H, dv)
    return o, s_fin


def _token_mixer(h, s_hg, s_gla, lb, w_in, gla_gk_up, gla_gk_bias, hg_out_norm, gla_out_norm, w_out):
    B, L, _ = h.shape
    f32 = jnp.float32
    split_idx = [int(i) for i in np.cumsum(IN_SIZES)[:-1]]
    z = jnp.einsum('bld,de->ble', h, w_in).astype(f32)
    hq, hf, hi, hgate, gq, gk, gv, ggate, glr = jnp.split(z, split_idx, axis=-1)
    lbf = lb.astype(f32)
    q_h = (jax.nn.silu(hq) * HG_DK ** -0.5).reshape(B, L, HG_HEADS, HG_DK)
    log_f = jnp.logaddexp(jnp.log(lbf), jnp.log1p(-lbf) + jax.nn.log_sigmoid(hf))
    k_h = (1.0 - lbf) * jax.nn.sigmoid(-hf)
    o_h, s_hg_new = _gated_linear_recurrence(
        q_h, k_h.reshape(B, L, HG_HEADS, HG_DK), hi.reshape(B, L, HG_HEADS, HG_DV),
        log_f.reshape(B, L, HG_HEADS, HG_DK), s_hg.astype(f32))
    o_h = _rmsnorm(o_h, hg_out_norm.reshape(HG_HEADS, HG_DV)).reshape(B, L, HG_WIDTH) * jax.nn.sigmoid(hgate)
    q_g = (gq * GLA_DK ** -0.5).reshape(B, L, GLA_HEADS, GLA_DK)
    log_a = jax.nn.log_sigmoid(glr @ gla_gk_up.astype(f32) + gla_gk_bias.astype(f32)) / GLA_GATE_NORMALIZER
    o_g, s_gla_new = _gated_linear_recurrence(
        q_g, gk.reshape(B, L, GLA_HEADS, GLA_DK), gv.reshape(B, L, GLA_HEADS, GLA_DV),
        log_a.reshape(B, L, GLA_HEADS, GLA_DK), s_gla.astype(f32))
    o_g = _rmsnorm(o_g, gla_out_norm).reshape(B, L, GLA_WIDTH) * jax.nn.silu(ggate)
    o = jnp.concatenate([o_h, o_g], axis=-1)
    out = jnp.einsum('ble,ed->bld', o, w_out.astype(f32)).astype(h.dtype)
    return out, s_hg_new.astype(s_hg.dtype), s_gla_new.astype(s_gla.dtype)


def _moe(h, w_router, b_router, w_gate, w_up, w_down, ws_gate, ws_up, ws_down):
    B, L, D = h.shape
    xt = h.reshape(-1, D)
    T = xt.shape[0]
    f32 = jnp.float32
    s = jax.nn.sigmoid(xt.astype(f32) @ w_router.astype(f32))
    sel = s + b_router.astype(f32)
    grp_score = lax.top_k(sel.reshape(T, N_GROUPS, N_EXPERTS // N_GROUPS), 2)[0].sum(-1)
    _, gidx = lax.top_k(grp_score, TOPK_GROUPS)
    gmask = jnp.any(gidx[:, :, None] == jnp.arange(N_GROUPS)[None, None, :], axis=1)
    emask = jnp.repeat(gmask, N_EXPERTS // N_GROUPS, axis=-1)
    _, eidx = lax.top_k(jnp.where(emask, sel, -jnp.inf), TOP_K)
    wsel = jnp.take_along_axis(s, eidx, axis=-1)
    wsel = wsel / jnp.sum(wsel, axis=-1, keepdims=True) * ROUTE_SCALE
    A = T * TOP_K
    flat_e = eidx.reshape(-1)
    flat_tok = jnp.repeat(jnp.arange(T, dtype=jnp.int32), TOP_K)
    order = jnp.argsort(flat_e)
    se, stok, sw = flat_e[order], flat_tok[order], wsel.reshape(-1)[order]
    counts = jax.ops.segment_sum(jnp.ones((A,), jnp.int32), flat_e, num_segments=N_EXPERTS)
    starts = jnp.cumsum(counts) - counts
    padded = (counts + MOE_BLOCK - 1) // MOE_BLOCK * MOE_BLOCK
    pends = jnp.cumsum(padded)
    pstarts = pends - padded
    dest = pstarts[se] + (jnp.arange(A, dtype=jnp.int32) - starts[se])
    nb = -(-(A + N_EXPERTS * (MOE_BLOCK - 1)) // MOE_BLOCK)
    P = nb * MOE_BLOCK
    row_tok = jnp.full((P,), T, jnp.int32).at[dest].set(stok)
    row_w = jnp.zeros((P,), f32).at[dest].set(sw)
    block_e = jnp.minimum(jnp.searchsorted(pends, jnp.arange(nb) * MOE_BLOCK, side='right'), N_EXPERTS - 1)
    xpad = jnp.concatenate([xt, jnp.zeros((1, D), xt.dtype)], axis=0)

    def body(y, inp):
        tok, wr, e = inp
        xb = xpad[tok]
        hb = jax.nn.silu(xb @ w_gate[e]) * (xb @ w_up[e])
        ob = (hb @ w_down[e]).astype(f32) * wr[:, None]
        return y.at[tok].add(ob), None

    y, _ = lax.scan(body, jnp.zeros((T + 1, D), f32),
                    (row_tok.reshape(nb, MOE_BLOCK), row_w.reshape(nb, MOE_BLOCK), block_e))
    shared = (jax.nn.silu(xt @ ws_gate) * (xt @ ws_up)) @ ws_down
    return (y[:T] + shared.astype(f32)).astype(h.dtype).reshape(B, L, D)


def _trunk(x, c, s_hg_all, s_gla_all, w_ada, b_ada, norm1, norm2, w_in, hg_lb_logits,
           gla_gk_up, gla_gk_bias, hg_out_norm, gla_out_norm, w_out, w_router, b_router,
           w_gate, w_up, w_down, ws_gate, ws_up, ws_down, norm_final):
    f32 = jnp.float32
    lb_all = jnp.cumsum(jax.nn.softmax(hg_lb_logits.astype(f32), axis=0), axis=0)
    new_hg, new_gla = [], []
    for l in range(DEPTH):
        mod = jax.nn.silu(c.astype(f32)) @ w_ada[l].astype(f32) + b_ada[l].astype(f32)
        sh1, sc1, g1, sh2, sc2, g2 = [m[:, None, :].astype(x.dtype) for m in jnp.split(mod, 6, axis=-1)]
        hmix = _rmsnorm(x, norm1[l]) * (1 + sc1) + sh1
        a, s_h, s_g = _token_mixer(hmix, s_hg_all[l], s_gla_all[l], lb_all[l], w_in[l], gla_gk_up[l],
                                   gla_gk_bias[l], hg_out_norm[l], gla_out_norm[l], w_out[l])
        x = x + g1 * a
        hffn = _rmsnorm(x, norm2[l]) * (1 + sc2) + sh2
        x = x + g2 * _moe(hffn, w_router[l], b_router[l], w_gate[l], w_up[l], w_down[l],
                          ws_gate[l], ws_up[l], ws_down[l])
        new_hg.append(s_h)
        new_gla.append(s_g)
    return _rmsnorm(x, norm_final), jnp.stack(new_hg), jnp.stack(new_gla)


def setup_inputs(seed: int = 0) -> dict:
    key = jax.random.key(seed)
    ks = jax.random.split(key, 26)
    D, E, F = D_MODEL, N_EXPERTS, D_EXPERT
    nrm = lambda k, shape, scale: jax.random.normal(k, shape, jnp.float32) * scale
    return {
        'x_prompt': nrm(ks[0], (BATCH, SEQ, D), 1.0),
        'x_sample': nrm(ks[1], (DEC_BATCH, DEC_SEQ, D), 1.0),
        'state_hgrn': nrm(ks[2], (DEPTH, DEC_BATCH, HG_HEADS, HG_DK, HG_DV), 0.3),
        'state_gla': nrm(ks[3], (DEPTH, DEC_BATCH, GLA_HEADS, GLA_DK, GLA_DV), 0.3),
        'c_prompt': nrm(ks[4], (BATCH, D), 1.0),
        'c_sample': nrm(ks[5], (DEC_BATCH, D), 1.0),
        'w_ada': nrm(ks[6], (DEPTH, D, 6 * D), 0.5 * D ** -0.5),
        'b_ada': nrm(ks[7], (DEPTH, 6 * D), 0.02),
        'norm1': 1.0 + nrm(ks[8], (DEPTH, D), 0.05),
        'norm2': 1.0 + nrm(ks[9], (DEPTH, D), 0.05),
        'w_in': nrm(ks[10], (DEPTH, D, D_IN), D ** -0.5),
        'hg_lb_logits': nrm(ks[11], (DEPTH + 1, HG_HEADS * HG_DK), 0.1),
        'gla_gk_up': nrm(ks[12], (DEPTH, GLA_GATE_RANK, GLA_HEADS * GLA_DK), GLA_GATE_RANK ** -0.5),
        'gla_gk_bias': nrm(ks[13], (DEPTH, GLA_HEADS * GLA_DK), 0.1),
        'hg_out_norm': 1.0 + nrm(ks[14], (DEPTH, HG_WIDTH), 0.05),
        'gla_out_norm': 1.0 + nrm(ks[15], (DEPTH, GLA_DV), 0.05),
        'w_out': nrm(ks[16], (DEPTH, MIX_WIDTH, D), MIX_WIDTH ** -0.5),
        'w_router': nrm(ks[17], (DEPTH, D, E), D ** -0.5),
        'b_router': nrm(ks[18], (DEPTH, E), 0.01),
        'w_gate': nrm(ks[19], (DEPTH, E, D, F), D ** -0.5),
        'w_up': nrm(ks[20], (DEPTH, E, D, F), D ** -0.5),
        'w_down': nrm(ks[21], (DEPTH, E, F, D), F ** -0.5),
        'ws_gate': nrm(ks[22], (DEPTH, D, F), D ** -0.5),
        'ws_up': nrm(ks[23], (DEPTH, D, F), D ** -0.5),
        'ws_down': nrm(ks[24], (DEPTH, F, D), F ** -0.5),
        'norm_final': 1.0 + nrm(ks[25], (D,), 0.05),
    }


def reference(x_prompt, x_sample, state_hgrn, state_gla, c_prompt, c_sample, w_ada, b_ada, norm1, norm2,
              w_in, hg_lb_logits, gla_gk_up, gla_gk_bias, hg_out_norm, gla_out_norm, w_out, w_router,
              b_router, w_gate, w_up, w_down, ws_gate, ws_up, ws_down, norm_final):
    weights = (w_ada, b_ada, norm1, norm2, w_in, hg_lb_logits, gla_gk_up, gla_gk_bias, hg_out_norm,
               gla_out_norm, w_out, w_router, b_router, w_gate, w_up, w_down, ws_gate, ws_up, ws_down,
               norm_final)
    nb = x_prompt.shape[0]
    zero_hg = jnp.zeros((DEPTH, nb, HG_HEADS, HG_DK, HG_DV), state_hgrn.dtype)
    zero_gla = jnp.zeros((DEPTH, nb, GLA_HEADS, GLA_DK, GLA_DV), state_gla.dtype)
    y_prompt, hg_prompt, gla_prompt = _trunk(x_prompt, c_prompt, zero_hg, zero_gla, *weights)
    y_sample, hg_sample, gla_sample = _trunk(x_sample, c_sample, state_hgrn, state_gla, *weights)
    return (y_prompt, y_sample, hg_prompt, gla_prompt, hg_sample, gla_sample)
```

```python
import functools

import jax
import jax.numpy as jnp
from jax import lax
from jax.experimental import pallas as pl
from jax.experimental.pallas import tpu as pltpu

f32 = jnp.float32
bf16 = jnp.bfloat16
i32 = jnp.int32
HIGHEST = lax.Precision.HIGHEST

EPS = 1e-6
CHUNK = 64
HG_HEADS = 4
GLA_HEADS = 4
GLA_GATE_NORMALIZER = 16.0
TOP_K = 8
N_GROUPS = 8
TOPK_GROUPS = 4
ROUTE_SCALE = 2.5

LANE = 128
SUBLANE = 8
V7X_VMEM_BYTES = 64 * 1024 * 1024
VMEM_LIMIT = V7X_VMEM_BYTES - 8 * 1024 * 1024

MIX_TILE = 512
ROUTE_TILE = 512
COMB_TILE = 256
MOE_BLOCK = 256
PAD_FILL = MOE_BLOCK + SUBLANE


def _cparams(*sem):
    return pltpu.CompilerParams(dimension_semantics=sem, vmem_limit_bytes=VMEM_LIMIT)


def _sigmoid_pair(x):
    e = jnp.exp(-jnp.abs(x))
    r = 1.0 / (1.0 + e)
    er = e * r
    pos = x >= 0
    return jnp.where(pos, r, er), jnp.where(pos, er, r)


def _rms(x):
    return x * lax.rsqrt(jnp.mean(x * x, axis=-1, keepdims=True) + EPS)


def _ada_body(c_ref, w_ref, b_ref, o_ref):
    c = c_ref[...]
    a = c * jax.nn.sigmoid(c)
    o_ref[...] = jnp.dot(a, w_ref[...], preferred_element_type=f32, precision=HIGHEST) + b_ref[...]


def _ada(c_all, w, b):
    ns, d = c_all.shape
    n = w.shape[1]
    tn = 1024
    return pl.pallas_call(
        _ada_body,
        grid=(n // tn,),
        in_specs=[pl.BlockSpec((ns, d), lambda j: (0, 0)),
                  pl.BlockSpec((d, tn), lambda j: (0, j)),
                  pl.BlockSpec((1, tn), lambda j: (0, j))],
        out_specs=pl.BlockSpec((ns, tn), lambda j: (0, j)),
        out_shape=jax.ShapeDtypeStruct((ns, n), f32),
        compiler_params=_cparams("arbitrary"),
        name="ada",
    )(c_all, w, b.reshape(1, n))


def _recurrence(q, k, v, g, st_ref, slot, head):
    c = q.shape[0]
    row = lax.broadcasted_iota(i32, (c, c), 0)
    col = lax.broadcasted_iota(i32, (c, c), 1)
    tril = (row >= col).astype(f32)
    b = jnp.dot(tril, g, preferred_element_type=f32, precision=HIGHEST)
    b_last = b[c - 1:c, :]
    qt = (q * jnp.exp(b)).astype(bf16)
    kt = (k * jnp.exp(-b)).astype(bf16)
    att = lax.dot_general(qt, kt, (((1,), (1,)), ((), ())), preferred_element_type=f32)
    att = jnp.where(row >= col, att, 0.0).astype(bf16)
    s_t = st_ref[slot, head]
    o = jnp.dot(att, v.astype(bf16), preferred_element_type=f32)
    o = o + lax.dot_general(qt, s_t.astype(bf16), (((1,), (1,)), ((), ())), preferred_element_type=f32)
    kd = (k * jnp.exp(b_last - b)).astype(bf16)
    upd = lax.dot_general(v.astype(bf16), kd, (((0,), (0,)), ((), ())), preferred_element_type=f32)
    st_ref[slot, head] = s_t * jnp.exp(b_last) + upd
    return o


def _mixer_body(x_ref, mod_ref, hg0_ref, gl0_ref, n1_ref, n2_ref, win_ref, lbl_ref, gup_ref, gbias_ref,
                hgn_ref, gln_ref, wout_ref, wrt_ref,
                x1_ref, hf_ref, lgt_ref, hg_ref, gl_ref,
                h_s, z_s, o_s, hgt_s, glt_s, *, nch, chunk_seq, tiles_per_seq):
    i = pl.program_id(0)
    d = x_ref.shape[1]
    hgw = HG_HEADS * hgt_s.shape[3]
    hdk, hdv = hgt_s.shape[3], hgt_s.shape[2]
    gdk, gdv = glt_s.shape[3], glt_s.shape[2]
    gkw = GLA_HEADS * gdk
    gvw = GLA_HEADS * gdv
    nslot = hgt_s.shape[0]

    def load_state():
        for s in range(nslot):
            for h in range(HG_HEADS):
                hgt_s[s, h] = hg0_ref[s, h].T
            for h in range(GLA_HEADS):
                glt_s[s, h] = gl0_ref[s, h].T

    if chunk_seq:
        load_state()
    else:
        pl.when(i % tiles_per_seq == 0)(load_state)

    lbl = lbl_ref[...]
    lbe = jnp.exp(lbl - jnp.max(lbl, axis=0, keepdims=True))
    lb = lbe[0:1, :] / jnp.sum(lbe, axis=0, keepdims=True)

    for j in range(nch):
        rows = pl.ds(j * CHUNK, CHUNK)
        m = mod_ref[j if chunk_seq else 0]
        xn = _rms(x_ref[rows, :]) * n1_ref[...]
        h_s[rows, :] = (xn * (1.0 + m[1:2, :]) + m[0:1, :]).astype(bf16)

    nz = z_s.shape[1]
    step = 512
    for n0 in range(0, nz, step):
        n1 = min(n0 + step, nz)
        z_s[:, n0:n1] = jnp.dot(h_s[...], win_ref[:, n0:n1], preferred_element_type=f32)

    o_hq, o_hf, o_hi, o_hgate = 0, hgw, 2 * hgw, 3 * hgw
    o_gq = 4 * hgw
    o_gk = o_gq + gkw
    o_gv = o_gk + gkw
    o_ggate = o_gv + gvw
    o_glr = o_ggate + gvw

    for j in range(nch):
        rows = pl.ds(j * CHUNK, CHUNK)
        slot = j if chunk_seq else 0
        for h in range(HG_HEADS):
            cs = pl.ds(h * hdk, hdk)
            lbh = lb[:, h * hdk:(h + 1) * hdk]
            hq = z_s[rows, pl.ds(o_hq + h * hdk, hdk)]
            hfg = z_s[rows, pl.ds(o_hf + h * hdk, hdk)]
            hi = z_s[rows, pl.ds(o_hi + h * hdv, hdv)]
            hgate = z_s[rows, pl.ds(o_hgate + h * hdv, hdv)]
            sig, nsig = _sigmoid_pair(hfg)
            g = jnp.log(lbh + (1.0 - lbh) * sig)
            kk = (1.0 - lbh) * nsig
            q = hq * jax.nn.sigmoid(hq) * (hdk ** -0.5)
            o = _recurrence(q, kk, hi, g, hgt_s, slot, h)
            o = _rms(o) * hgn_ref[:, pl.ds(h * hdv, hdv)] * jax.nn.sigmoid(hgate)
            o_s[rows, pl.ds(h * hdv, hdv)] = o.astype(bf16)
            del cs
        glr = z_s[rows, pl.ds(o_glr, LANE)]
        u = jnp.dot(glr, gup_ref[...], preferred_element_type=f32, precision=HIGHEST) + gbias_ref[...]
        loga = (jnp.minimum(u, 0.0) - jnp.log1p(jnp.exp(-jnp.abs(u)))) * (1.0 / GLA_GATE_NORMALIZER)
        for h in range(GLA_HEADS):
            gq = z_s[rows, pl.ds(o_gq + h * gdk, gdk)] * (gdk ** -0.5)
            gk = z_s[rows, pl.ds(o_gk + h * gdk, gdk)]
            gv = z_s[rows, pl.ds(o_gv + h * gdv, gdv)]
            ggate = z_s[rows, pl.ds(o_ggate + h * gdv, gdv)]
            o = _recurrence(gq, gk, gv, loga[:, h * gdk:(h + 1) * gdk], glt_s, slot, h)
            o = _rms(o) * gln_ref[...] * (ggate * jax.nn.sigmoid(ggate))
            o_s[rows, pl.ds(hgw + h * gdv, gdv)] = o.astype(bf16)

    a = jnp.dot(o_s[...], wout_ref[...], preferred_element_type=f32)
    for j in range(nch):
        rows = pl.ds(j * CHUNK, CHUNK)
        m = mod_ref[j if chunk_seq else 0]
        x1 = x_ref[rows, :] + m[2:3, :] * a[j * CHUNK:(j + 1) * CHUNK, :]
        x1_ref[rows, :] = x1
        hf_ref[rows, :] = _rms(x1) * n2_ref[...] * (1.0 + m[4:5, :]) + m[3:4, :]
    lgt_ref[...] = lax.dot_general(wrt_ref[...], hf_ref[...], (((1,), (1,)), ((), ())),
                                   preferred_element_type=f32, precision=HIGHEST)

    def store_state():
        for s in range(nslot):
            for h in range(HG_HEADS):
                hg_ref[s, h] = hgt_s[s, h].T
            for h in range(GLA_HEADS):
                gl_ref[s, h] = glt_s[s, h].T

    if chunk_seq:
        store_state()
    else:
        pl.when(i % tiles_per_seq == tiles_per_seq - 1)(store_state)


def _mixer(x, mod, hg0, gl0, p, chunk_seq):
    nseq, L, d = x.shape
    t = nseq * L
    tm = MIX_TILE
    nch = tm // CHUNK
    if chunk_seq:
        assert L == CHUNK and nseq % nch == 0
        spt, tiles_per_seq = nch, 1
        seq_of = lambda i: i
    else:
        assert L % tm == 0
        spt, tiles_per_seq = 1, L // tm
        seq_of = lambda i: i // tiles_per_seq
    ne = p["w_rt"].shape[0]
    nz = p["w_in"].shape[1]
    _, hh, hdk, hdv = hg0.shape
    _, gh, gdk, gdv = gl0.shape
    const = lambda shape: pl.BlockSpec(shape, lambda i: (0,) * len(shape))
    body = functools.partial(_mixer_body, nch=nch, chunk_seq=chunk_seq, tiles_per_seq=tiles_per_seq)
    return pl.pallas_call(
        body,
        grid=(t // tm,),
        in_specs=[pl.BlockSpec((tm, d), lambda i: (i, 0)),
                  pl.BlockSpec((spt, 6, d), lambda i: (seq_of(i), 0, 0)),
                  pl.BlockSpec((spt, hh, hdk, hdv), lambda i: (seq_of(i), 0, 0, 0)),
                  pl.BlockSpec((spt, gh, gdk, gdv), lambda i: (seq_of(i), 0, 0, 0)),
                  const((1, d)), const((1, d)), const((d, nz)), const(p["lb_logits"].shape),
                  const(p["gk_up"].shape), const(p["gk_bias"].shape), const(p["hg_norm"].shape),
                  const(p["gla_norm"].shape), const((d, d)), const((ne, d))],
        out_specs=[pl.BlockSpec((tm, d), lambda i: (i, 0)),
                   pl.BlockSpec((tm, d), lambda i: (i, 0)),
                   pl.BlockSpec((ne, tm), lambda i: (0, i)),
                   pl.BlockSpec((spt, hh, hdk, hdv), lambda i: (seq_of(i), 0, 0, 0)),
                   pl.BlockSpec((spt, gh, gdk, gdv), lambda i: (seq_of(i), 0, 0, 0))],
        out_shape=[jax.ShapeDtypeStruct((t, d), f32),
                   jax.ShapeDtypeStruct((t, d), f32),
                   jax.ShapeDtypeStruct((ne, t), f32),
                   jax.ShapeDtypeStruct(hg0.shape, f32),
                   jax.ShapeDtypeStruct(gl0.shape, f32)],
        scratch_shapes=[pltpu.VMEM((tm, d), bf16),
                        pltpu.VMEM((tm, nz), f32),
                        pltpu.VMEM((tm, d), bf16),
                        pltpu.VMEM((spt, hh, hdv, hdk), f32),
                        pltpu.VMEM((spt, gh, gdv, gdk), f32)],
        compiler_params=_cparams("arbitrary"),
        name="mixer_sample" if chunk_seq else "mixer_prompt",
    )(x.reshape(t, d), mod, hg0, gl0, p["norm1"], p["norm2"], p["w_in"], p["lb_logits"], p["gk_up"],
      p["gk_bias"], p["hg_norm"], p["gla_norm"], p["w_out"], p["w_rt"])


def _route_body(lg_ref, bias_ref, eidx_ref, wsel_ref, rank_ref, cnt_ref, run_s, s_s, cand_s):
    i = pl.program_id(0)
    ne, tt = lg_ref.shape
    gsz = ne // N_GROUPS
    ninf = -jnp.inf

    @pl.when(i == 0)
    def _():
        run_s[...] = jnp.zeros_like(run_s)

    row = lax.broadcasted_iota(i32, (ne, tt), 0)
    rg = lax.broadcasted_iota(i32, (gsz, tt), 0)

    gs = []
    for g in range(N_GROUPS):
        grows = pl.ds(g * gsz, gsz)
        sg = jax.nn.sigmoid(lg_ref[grows, :])
        s_s[grows, :] = sg
        blk = sg + bias_ref[grows, :]
        m1 = jnp.max(blk, axis=0, keepdims=True)
        first = jnp.min(jnp.where(blk == m1, rg, gsz), axis=0, keepdims=True)
        m2 = jnp.max(jnp.where(rg == first, ninf, blk), axis=0, keepdims=True)
        gs.append(m1 + m2)

    picked = [jnp.zeros((1, tt), f32) for _ in range(N_GROUPS)]
    for _ in range(TOPK_GROUPS):
        cur = [jnp.where(picked[g] > 0.5, ninf, gs[g]) for g in range(N_GROUPS)]
        m = functools.reduce(jnp.maximum, cur)
        gi = jnp.full((1, tt), N_GROUPS, i32)
        for g in reversed(range(N_GROUPS)):
            gi = jnp.where(cur[g] == m, g, gi)
        picked = [jnp.where(gi == g, 1.0, picked[g]) for g in range(N_GROUPS)]

    for g in range(N_GROUPS):
        grows = pl.ds(g * gsz, gsz)
        allowed = jnp.broadcast_to(picked[g], (gsz, tt)) > 0.5
        cand_s[grows, :] = jnp.where(allowed, s_s[grows, :] + bias_ref[grows, :], ninf)
    cand = cand_s[...]
    s = s_s[...]

    eis, ws = [], []
    chosen = jnp.zeros((ne, tt), jnp.bool_)
    for _ in range(TOP_K):
        m = jnp.max(cand, axis=0, keepdims=True)
        ei = jnp.min(jnp.where(cand == m, row, ne), axis=0, keepdims=True)
        hit = row == ei
        ws.append(jnp.sum(jnp.where(hit, s, 0.0), axis=0, keepdims=True))
        cand = jnp.where(hit, ninf, cand)
        chosen = chosen | hit
        eis.append(ei)
    wsum = functools.reduce(jnp.add, ws)
    scale = ROUTE_SCALE / wsum

    onehot = chosen.astype(bf16)
    ti = lax.broadcasted_iota(i32, (tt, tt), 0)
    tj = lax.broadcasted_iota(i32, (tt, tt), 1)
    before = (ti < tj).astype(bf16)
    prior = jnp.dot(onehot, before, preferred_element_type=f32) + run_s[:, 0:1]
    for k in range(TOP_K):
        hit = row == eis[k]
        eidx_ref[k:k + 1, :] = eis[k]
        wsel_ref[k:k + 1, :] = ws[k] * scale
        rank_ref[k:k + 1, :] = jnp.sum(jnp.where(hit, prior, 0.0), axis=0, keepdims=True).astype(i32)
    run_s[...] = run_s[...] + jnp.sum(chosen.astype(f32), axis=1, keepdims=True)
    cnt_ref[...] = run_s[...].astype(i32)


def _route(lgt, b_router):
    ne, t = lgt.shape
    tt = ROUTE_TILE
    tok = lambda dt: jax.ShapeDtypeStruct((TOP_K, t), dt)
    return pl.pallas_call(
        _route_body,
        grid=(t // tt,),
        in_specs=[pl.BlockSpec((ne, tt), lambda i: (0, i)),
                  pl.BlockSpec((ne, 1), lambda i: (0, 0))],
        out_specs=[pl.BlockSpec((TOP_K, tt), lambda i: (0, i)),
                   pl.BlockSpec((TOP_K, tt), lambda i: (0, i)),
                   pl.BlockSpec((TOP_K, tt), lambda i: (0, i)),
                   pl.BlockSpec((ne, LANE), lambda i: (0, 0))],
        out_shape=[tok(i32), tok(f32), tok(i32), jax.ShapeDtypeStruct((ne, LANE), i32)],
        scratch_shapes=[pltpu.VMEM((ne, LANE), f32), pltpu.VMEM((ne, tt), f32), pltpu.VMEM((ne, tt), f32)],
        compiler_params=_cparams("arbitrary"),
        name="route",
    )(lgt, b_router.reshape(ne, 1))


def _dispatch_body(dest_ref, padstart_ref, hf_ref, xs_ref, zero_s, sem, zsem):
    i = pl.program_id(0)
    tt = hf_ref.shape[0]
    ne = padstart_ref.shape[0]
    nfill = zero_s.shape[0]

    @pl.when(i == 0)
    def _():
        zero_s[...] = jnp.zeros_like(zero_s)

        def fill(e, c):
            start = pl.multiple_of(padstart_ref[e] // SUBLANE * SUBLANE, SUBLANE)
            pltpu.make_async_copy(zero_s, xs_ref.at[pl.ds(start, nfill), :], zsem).start()
            return c

        lax.fori_loop(0, ne, fill, 0)

        def drain(e, c):
            pltpu.make_async_copy(zero_s, xs_ref.at[pl.ds(0, nfill), :], zsem).wait()
            return c

        lax.fori_loop(0, ne, drain, 0)

    def send(t, c):
        for k in range(TOP_K):
            pltpu.make_async_copy(hf_ref.at[pl.ds(t, 1), :],
                                  xs_ref.at[pl.ds(dest_ref[0, k, t], 1), :], sem).start()
        return c

    lax.fori_loop(0, tt, send, 0)
    for k in range(TOP_K):
        pltpu.make_async_copy(hf_ref, xs_ref.at[pl.ds(0, tt), :], sem).wait()


def _dispatch(dest3, padstart, hf, p_rows):
    t, d = hf.shape
    tt = ROUTE_TILE
    return pl.pallas_call(
        _dispatch_body,
        grid=(t // tt,),
        in_specs=[pl.BlockSpec((1, TOP_K, tt), lambda i: (i, 0, 0), memory_space=pltpu.SMEM),
                  pl.BlockSpec(memory_space=pltpu.SMEM),
                  pl.BlockSpec((tt, d), lambda i: (i, 0))],
        out_specs=pl.BlockSpec(memory_space=pl.ANY),
        out_shape=jax.ShapeDtypeStruct((p_rows + PAD_FILL, d), f32),
        scratch_shapes=[pltpu.VMEM((PAD_FILL, d), f32), pltpu.SemaphoreType.DMA, pltpu.SemaphoreType.DMA],
        compiler_params=_cparams("arbitrary"),
        name="dispatch",
    )(dest3, padstart, hf)


def _experts_body(be_ref, nused_ref, xs_ref, wg_ref, wu_ref, wd_ref, eo_ref, wgu_s, wd_s):
    i = pl.program_id(0)
    f = wg_ref.shape[2]
    live = i < nused_ref[0]
    fresh = jnp.logical_or(i == 0, be_ref[i] != be_ref[jnp.maximum(i - 1, 0)])

    @pl.when(jnp.logical_and(live, fresh))
    def _():
        wgu_s[:, 0:f] = wg_ref[0].astype(bf16)
        wgu_s[:, f:2 * f] = wu_ref[0].astype(bf16)
        wd_s[...] = wd_ref[0].astype(bf16)

    @pl.when(live)
    def _():
        x = xs_ref[...].astype(bf16)
        gu = jnp.dot(x, wgu_s[...], preferred_element_type=f32)
        g = gu[:, 0:f]
        h = (g * jax.nn.sigmoid(g) * gu[:, f:2 * f]).astype(bf16)
        eo_ref[...] = jnp.dot(h, wd_s[...], preferred_element_type=f32)


def _experts(block_e, nused, xs, w_gate, w_up, w_down, nb):
    d = xs.shape[1]
    ne, _, f = w_gate.shape
    blk = MOE_BLOCK
    row_map = lambda i, be, nu: (jnp.minimum(i, nu[0] - 1), 0)
    w_map = lambda i, be, nu: (be[i], 0, 0)
    grid_spec = pltpu.PrefetchScalarGridSpec(
        num_scalar_prefetch=2,
        grid=(nb,),
        in_specs=[pl.BlockSpec((blk, d), row_map),
                  pl.BlockSpec((1, d, f), w_map),
                  pl.BlockSpec((1, d, f), w_map),
                  pl.BlockSpec((1, f, d), w_map)],
        out_specs=pl.BlockSpec((blk, d), row_map),
        scratch_shapes=[pltpu.VMEM((d, 2 * f), bf16), pltpu.VMEM((f, d), bf16)],
    )
    return pl.pallas_call(
        _experts_body,
        grid_spec=grid_spec,
        out_shape=jax.ShapeDtypeStruct((nb * blk, d), f32),
        compiler_params=_cparams("arbitrary"),
        name="experts",
    )(block_e, nused, xs, w_gate, w_up, w_down)


def _combine_body(dest_ref, eo_ref, x1_ref, hf_ref, w_ref, mod_ref, wsg_ref, wsu_ref, wsd_ref, nf_ref,
                  y_ref, buf, sem, *, tiles_per_seq_prompt, n_prompt_tiles, seqs_per_tile_sample, n_prompt_seq):
    i = pl.program_id(0)
    tt = x1_ref.shape[0]
    f = wsg_ref.shape[1]

    def fetch(t, c):
        for k in range(TOP_K):
            pltpu.make_async_copy(eo_ref.at[pl.ds(dest_ref[0, k, t], 1), :],
                                  buf.at[pl.ds(k * tt + t, 1), :], sem).start()
        return c

    lax.fori_loop(0, tt, fetch, 0)

    hf = hf_ref[...].astype(bf16)
    g = jnp.dot(hf, wsg_ref[...], preferred_element_type=f32)
    u = jnp.dot(hf, wsu_ref[...], preferred_element_type=f32)
    acc = jnp.dot((g * jax.nn.sigmoid(g) * u).astype(bf16), wsd_ref[...], preferred_element_type=f32)

    for k in range(TOP_K):
        pltpu.make_async_copy(eo_ref.at[pl.ds(0, tt), :], buf.at[pl.ds(k * tt, tt), :], sem).wait()
    for k in range(TOP_K):
        acc = acc + buf[pl.ds(k * tt, tt), :] * w_ref[:, k:k + 1]

    is_prompt = i < n_prompt_tiles
    nch = tt // CHUNK
    for j in range(nch):
        rows = pl.ds(j * CHUNK, CHUNK)
        seq = jnp.where(is_prompt, i // tiles_per_seq_prompt,
                        n_prompt_seq + (i - n_prompt_tiles) * seqs_per_tile_sample + j)
        g2 = mod_ref[seq][5:6, :]
        y = x1_ref[rows, :] + g2 * acc[j * CHUNK:(j + 1) * CHUNK, :]
        y_ref[rows, :] = _rms(y) * nf_ref[...]


def _combine(dest3, eo, x1, hf, wsel_t, mod, p, n_prompt_seq, prompt_len, sample_len):
    t, d = x1.shape
    tt = COMB_TILE
    f = p["ws_gate"].shape[1]
    ns = mod.shape[0]
    assert sample_len == CHUNK and prompt_len % tt == 0
    n_prompt_tiles = n_prompt_seq * prompt_len // tt
    body = functools.partial(_combine_body, tiles_per_seq_prompt=prompt_len // tt,
                             n_prompt_tiles=n_prompt_tiles, seqs_per_tile_sample=tt // CHUNK,
                             n_prompt_seq=n_prompt_seq)
    const = lambda shape: pl.BlockSpec(shape, lambda i: (0,) * len(shape))
    return pl.pallas_call(
        body,
        grid=(t // tt,),
        in_specs=[pl.BlockSpec((1, TOP_K, tt), lambda i: (i, 0, 0), memory_space=pltpu.SMEM),
                  pl.BlockSpec(memory_space=pl.ANY),
                  pl.BlockSpec((tt, d), lambda i: (i, 0)),
                  pl.BlockSpec((tt, d), lambda i: (i, 0)),
                  pl.BlockSpec((tt, TOP_K), lambda i: (i, 0)),
                  const((ns, 6, d)), const((d, f)), const((d, f)), const((f, d)), const((1, d))],
        out_specs=pl.BlockSpec((tt, d), lambda i: (i, 0)),
        out_shape=jax.ShapeDtypeStruct((t, d), f32),
        scratch_shapes=[pltpu.VMEM((TOP_K * tt, d), f32), pltpu.SemaphoreType.DMA],
        compiler_params=_cparams("arbitrary"),
        name="combine",
    )(dest3, eo, x1, hf, wsel_t, mod, p["ws_gate"], p["ws_up"], p["ws_down"], p["norm_final"])


def kernel(x_prompt, x_sample, state_hgrn, state_gla, c_prompt, c_sample, w_ada, b_ada, norm1, norm2, w_in,
           hg_lb_logits, gla_gk_up, gla_gk_bias, hg_out_norm, gla_out_norm, w_out, w_router, b_router,
           w_gate, w_up, w_down, ws_gate, ws_up, ws_down, norm_final):
    nbp, lp, d = x_prompt.shape
    nbs, ls, _ = x_sample.shape
    depth = w_in.shape[0]
    assert depth == 1
    ne = w_router.shape[2]
    hgw = hg_out_norm.shape[1]
    gkw = gla_gk_up.shape[2]
    rank = gla_gk_up.shape[1]
    d_in = w_in.shape[2]

    nz = d_in - rank + LANE
    w_in_p = jnp.zeros((d, nz), bf16).at[:, :d_in].set(w_in[0].astype(bf16))
    p = dict(
        norm1=norm1[0].reshape(1, d), norm2=norm2[0].reshape(1, d), w_in=w_in_p,
        lb_logits=hg_lb_logits,
        gk_up=jnp.zeros((LANE, gkw), f32).at[:rank].set(gla_gk_up[0]),
        gk_bias=gla_gk_bias[0].reshape(1, gkw),
        hg_norm=hg_out_norm[0].reshape(1, hgw), gla_norm=gla_out_norm[0].reshape(1, -1),
        w_out=w_out[0].astype(bf16), w_rt=w_router[0].T,
        ws_gate=ws_gate[0].astype(bf16), ws_up=ws_up[0].astype(bf16), ws_down=ws_down[0].astype(bf16),
        norm_final=norm_final.reshape(1, d),
    )

    c_all = jnp.concatenate([c_prompt, c_sample], axis=0)
    mod = _ada(c_all, w_ada[0], b_ada[0]).reshape(nbp + nbs, 6, d)

    zero_hg = jnp.zeros((nbp,) + state_hgrn.shape[2:], f32)
    zero_gl = jnp.zeros((nbp,) + state_gla.shape[2:], f32)
    x1p, hfp, lgp, hg_p, gl_p = _mixer(x_prompt, mod[:nbp], zero_hg, zero_gl, p, chunk_seq=False)
    x1s, hfs, lgs, hg_s, gl_s = _mixer(x_sample, mod[nbp:], state_hgrn[0], state_gla[0], p, chunk_seq=True)
    x1 = jnp.concatenate([x1p, x1s], axis=0)
    hf = jnp.concatenate([hfp, hfs], axis=0)
    lgt = jnp.concatenate([lgp, lgs], axis=1)
    t = x1.shape[0]

    eidx_t, wsel_t, rank_t, cnt = _route(lgt, b_router[0])

    counts = cnt[:, 0]
    padded = (counts + MOE_BLOCK - 1) // MOE_BLOCK * MOE_BLOCK
    pends = jnp.cumsum(padded)
    pstarts = pends - padded
    a_total = t * TOP_K
    nb = -(-(a_total + ne * (MOE_BLOCK - 1)) // MOE_BLOCK)
    dest_t = pstarts[eidx_t] + rank_t
    block_e = jnp.minimum(jnp.searchsorted(pends, jnp.arange(nb, dtype=i32) * MOE_BLOCK, side="right"),
                          ne - 1).astype(i32)
    nused = (pends[-1] // MOE_BLOCK).astype(i32).reshape(1)

    def tiles(a, tt):
        return a.reshape(TOP_K, t // tt, tt).transpose(1, 0, 2)

    xs = _dispatch(tiles(dest_t, ROUTE_TILE), (pstarts + counts).astype(i32), hf, nb * MOE_BLOCK)
    eo = _experts(block_e, nused, xs, w_gate[0], w_up[0], w_down[0], nb)
    y = _combine(tiles(dest_t, COMB_TILE), eo, x1, hf, wsel_t.T, mod, p, nbp, lp, ls)

    y_prompt = y[:nbp * lp].reshape(nbp, lp, d)
    y_sample = y[nbp * lp:].reshape(nbs, ls, d)
    return (y_prompt, y_sample, hg_p[None], gl_p[None], hg_s[None], gl_s[None])
```

```python
import functools

import jax
import jax.numpy as jnp
from jax import lax
from jax.experimental import pallas as pl
from jax.experimental.pallas import tpu as pltpu

f32 = jnp.float32
bf16 = jnp.bfloat16
i32 = jnp.int32
HIGHEST = lax.Precision.HIGHEST

EPS = 1e-6
CHUNK = 64
HG_HEADS = 4
GLA_HEADS = 4
GLA_GATE_NORMALIZER = 16.0
TOP_K = 8
N_GROUPS = 8
TOPK_GROUPS = 4
ROUTE_SCALE = 2.5

LANE = 128
SUBLANE = 8
V7X_VMEM_BYTES = 64 * 1024 * 1024
VMEM_LIMIT = V7X_VMEM_BYTES - 8 * 1024 * 1024

MIX_TILE = 512
ROUTE_TILE = 512
COMB_TILE = 256
MOE_BLOCK = 256


def _cparams(*sem):
    return pltpu.CompilerParams(dimension_semantics=sem, vmem_limit_bytes=VMEM_LIMIT)


def _sigmoid_pair(x):
    e = jnp.exp(-jnp.abs(x))
    r = 1.0 / (1.0 + e)
    er = e * r
    pos = x >= 0
    return jnp.where(pos, r, er), jnp.where(pos, er, r)


def _rms(x):
    return x * lax.rsqrt(jnp.mean(x * x, axis=-1, keepdims=True) + EPS)


NSLAB = SUBLANE


def _slab(ref, row0, n):
    return ref.at[pl.ds(pl.multiple_of(row0 * NSLAB, NSLAB), n * NSLAB), :]


def _load_rows(ref, row0, n):
    return jnp.concatenate([ref[pl.ds(row0 * NSLAB + s, n, stride=NSLAB), :] for s in range(NSLAB)], axis=1)


def _store_rows(ref, row0, x):
    for s in range(NSLAB):
        ref[pl.ds(row0 * NSLAB + s, x.shape[0], stride=NSLAB), :] = x[:, s * LANE:(s + 1) * LANE]


def _ada_body(c_ref, w_ref, b_ref, o_ref):
    c = c_ref[...]
    a = c * jax.nn.sigmoid(c)
    o_ref[...] = jnp.dot(a, w_ref[...], preferred_element_type=f32, precision=HIGHEST) + b_ref[...]


def _ada(c_all, w, b):
    ns, d = c_all.shape
    n = w.shape[1]
    tn = 1024
    return pl.pallas_call(
        _ada_body,
        grid=(n // tn,),
        in_specs=[pl.BlockSpec((ns, d), lambda j: (0, 0)),
                  pl.BlockSpec((d, tn), lambda j: (0, j)),
                  pl.BlockSpec((1, tn), lambda j: (0, j))],
        out_specs=pl.BlockSpec((ns, tn), lambda j: (0, j)),
        out_shape=jax.ShapeDtypeStruct((ns, n), f32),
        compiler_params=_cparams("arbitrary"),
        name="ada",
    )(c_all, w, b.reshape(1, n))


LEVELS = tuple(1 << n for n in range(CHUNK.bit_length() - 1))
NT = (((1,), (1,)), ((), ()))
TN = (((0,), (0,)), ((), ()))


def _pair_level(c):
    t = lax.broadcasted_iota(i32, (c, c), 0)
    s = lax.broadcasted_iota(i32, (c, c), 1)
    x = t ^ s
    lvl = jnp.full((c, c), -1, i32)
    for li, m in enumerate(LEVELS):
        lvl = jnp.where(x >= m, li, lvl)
    return jnp.where(t < s, -2, lvl)


def _chunk_scan(q, k, g):
    c = q.shape[0]
    row = lax.broadcasted_iota(i32, q.shape, 0)
    pre = g
    tot = g
    zs = []
    for m in LEVELS:
        upper = (row & (2 * m - 1)) >= m
        e = jnp.exp(jnp.where(upper, pre, tot - pre))
        zs.append((jnp.where(upper, q, k) * e).astype(bf16))
        below = pltpu.roll(tot, m, 0)
        above = pltpu.roll(tot, c - m, 0)
        pre = pre + jnp.where(upper, below, 0.0)
        tot = tot + jnp.where(upper, below, above)
    qb = (q * jnp.exp(pre)).astype(bf16)
    kd = (k * jnp.exp(tot - pre)).astype(bf16)
    return zs, qb, kd, tot[0:1, :]


def _chunk_head(zs, qd, kd0, qb, kd, decay, v, st_ref, slot, head, keep, lvl):
    own = (lambda a: a) if keep is None else (lambda a: a * keep)
    att = jnp.where(lvl == -1, lax.dot_general(own(qd), kd0, NT, preferred_element_type=f32), 0.0)
    for li, z in enumerate(zs):
        att = att + jnp.where(lvl == li, lax.dot_general(own(z), z, NT, preferred_element_type=f32), 0.0)
    st = st_ref[slot, head]
    o = jnp.dot(att.astype(bf16), v, preferred_element_type=f32)
    o = o + lax.dot_general(own(qb), st.astype(bf16), NT, preferred_element_type=f32)
    st_ref[slot, head] = st * decay + lax.dot_general(v, own(kd), TN, preferred_element_type=f32)
    return o


def _mixer_body(*refs, nch, chunk_seq, tiles_per_seq, nprev):
    (x_ref, mod_ref, hg0_ref, gl0_ref, n1_ref, n2_ref, win_ref, lbl_ref, gup_ref, gbias_ref,
     hgn_ref, gln_ref, wout_ref, wrt_ref) = refs[:14]
    (x1_ref, hp_ref, lgt_ref, hg_ref, gl_ref, h_s, z_s, o_s, hf_s, hgt_s, glt_s) = refs[14 + nprev:]
    i = pl.program_id(0)
    d = x_ref.shape[1]
    hdk, hdv = hg0_ref.shape[2], hg0_ref.shape[3]
    gdk, gdv = gl0_ref.shape[2], gl0_ref.shape[3]
    assert hdk == LANE and LANE % gdk == 0
    hgw = HG_HEADS * hdk
    gkw = GLA_HEADS * gdk
    gvw = GLA_HEADS * gdv
    nslot = hgt_s.shape[0]
    gla_lanes = lambda h: pl.ds((h * gdk) % LANE, gdk)

    def load_state():
        for s in range(nslot):
            for h in range(HG_HEADS):
                hgt_s[s, h] = hg0_ref[s, h].T
            for h in range(GLA_HEADS):
                glt_s[s, h] = jnp.zeros(glt_s.shape[2:], f32)
                glt_s[s, h, :, gla_lanes(h)] = gl0_ref[s, h].T

    if chunk_seq:
        load_state()
    else:
        pl.when(i % tiles_per_seq == 0)(load_state)

    lbl = lbl_ref[...]
    lbe = jnp.exp(lbl - jnp.max(lbl, axis=0, keepdims=True))
    lb = lbe[0:1, :] / jnp.sum(lbe, axis=0, keepdims=True)

    for j in range(nch):
        rows = pl.ds(j * CHUNK, CHUNK)
        m = mod_ref[j if chunk_seq else 0]
        xn = _rms(x_ref[rows, :]) * n1_ref[...]
        h_s[rows, :] = (xn * (1.0 + m[1:2, :]) + m[0:1, :]).astype(bf16)

    nz = z_s.shape[1]
    step = 512
    for n0 in range(0, nz, step):
        n1 = min(n0 + step, nz)
        z_s[:, n0:n1] = jnp.dot(h_s[...], win_ref[:, n0:n1], preferred_element_type=f32)

    o_hq, o_hf, o_hi, o_hgate = 0, hgw, 2 * hgw, 3 * hgw
    o_gq = 4 * hgw
    o_gk = o_gq + gkw
    o_gv = o_gk + gkw
    o_ggate = o_gv + gvw
    o_glr = o_ggate + gvw

    lvl = _pair_level(CHUNK)
    lane = lax.broadcasted_iota(i32, (CHUNK, LANE), 1)
    heads_per_block = LANE // gdk
    for j in range(nch):
        rows = pl.ds(j * CHUNK, CHUNK)
        slot = j if chunk_seq else 0
        for h in range(HG_HEADS):
            lbh = lb[:, h * hdk:(h + 1) * hdk]
            hq = z_s[rows, pl.ds(o_hq + h * hdk, hdk)]
            sig, nsig = _sigmoid_pair(z_s[rows, pl.ds(o_hf + h * hdk, hdk)])
            g = jnp.log(lbh + (1.0 - lbh) * sig)
            k = (1.0 - lbh) * nsig
            q = hq * jax.nn.sigmoid(hq) * (hdk ** -0.5)
            zs, qb, kd, b_last = _chunk_scan(q, k, g)
            v = z_s[rows, pl.ds(o_hi + h * hdv, hdv)].astype(bf16)
            o = _chunk_head(zs, q.astype(bf16), k.astype(bf16), qb, kd, jnp.exp(b_last), v,
                            hgt_s, slot, h, None, lvl)
            hgate = z_s[rows, pl.ds(o_hgate + h * hdv, hdv)]
            o = _rms(o) * hgn_ref[:, pl.ds(h * hdv, hdv)] * jax.nn.sigmoid(hgate)
            o_s[rows, pl.ds(h * hdv, hdv)] = o.astype(bf16)
        glr = z_s[rows, pl.ds(o_glr, LANE)]
        u = jnp.dot(glr, gup_ref[...], preferred_element_type=f32, precision=HIGHEST) + gbias_ref[...]
        loga = (jnp.minimum(u, 0.0) - jnp.log1p(jnp.exp(-jnp.abs(u)))) * (1.0 / GLA_GATE_NORMALIZER)
        for blk in range(GLA_HEADS // heads_per_block):
            q = z_s[rows, pl.ds(o_gq + blk * LANE, LANE)] * (gdk ** -0.5)
            k = z_s[rows, pl.ds(o_gk + blk * LANE, LANE)]
            zs, qb, kd, b_last = _chunk_scan(q, k, loga[:, blk * LANE:(blk + 1) * LANE])
            qd, kd0, decay = q.astype(bf16), k.astype(bf16), jnp.exp(b_last)
            for h in range(blk * heads_per_block, (blk + 1) * heads_per_block):
                lo = (h * gdk) % LANE
                keep = ((lane >= lo) & (lane < lo + gdk)).astype(bf16)
                v = z_s[rows, pl.ds(o_gv + h * gdv, gdv)].astype(bf16)
                o = _chunk_head(zs, qd, kd0, qb, kd, decay, v, glt_s, slot, h, keep, lvl)
                ggate = z_s[rows, pl.ds(o_ggate + h * gdv, gdv)]
                o = _rms(o) * gln_ref[...] * (ggate * jax.nn.sigmoid(ggate))
                o_s[rows, pl.ds(hgw + h * gdv, gdv)] = o.astype(bf16)

    a = jnp.dot(o_s[...], wout_ref[...], preferred_element_type=f32)
    for j in range(nch):
        rows = pl.ds(j * CHUNK, CHUNK)
        m = mod_ref[j if chunk_seq else 0]
        x1 = x_ref[rows, :] + m[2:3, :] * a[j * CHUNK:(j + 1) * CHUNK, :]
        x1_ref[rows, :] = x1
        hf = _rms(x1) * n2_ref[...] * (1.0 + m[4:5, :]) + m[3:4, :]
        hf_s[rows, :] = hf
        _store_rows(hp_ref, j * CHUNK, hf)
    lgt_ref[...] = lax.dot_general(wrt_ref[...], hf_s[...], (((1,), (1,)), ((), ())),
                                   preferred_element_type=f32, precision=HIGHEST)

    def store_state():
        for s in range(nslot):
            for h in range(HG_HEADS):
                hg_ref[s, h] = hgt_s[s, h].T
            for h in range(GLA_HEADS):
                gl_ref[s, h] = glt_s[s, h, :, gla_lanes(h)].T

    if chunk_seq:
        store_state()
    else:
        pl.when(i % tiles_per_seq == tiles_per_seq - 1)(store_state)


def _mixer(x, mod, hg0, gl0, p, chunk_seq, t_total, tile_offset, prev=()):
    nseq, L, d = x.shape
    t = nseq * L
    tm = MIX_TILE
    nch = tm // CHUNK
    if chunk_seq:
        assert L == CHUNK and nseq % nch == 0
        spt, tiles_per_seq = nch, 1
        seq_of = lambda i: i
    else:
        assert L % tm == 0
        spt, tiles_per_seq = 1, L // tm
        seq_of = lambda i: i // tiles_per_seq
    ne = p["w_rt"].shape[0]
    nz = p["w_in"].shape[1]
    _, hh, hdk, hdv = hg0.shape
    _, gh, gdk, gdv = gl0.shape
    const = lambda shape: pl.BlockSpec(shape, lambda i: (0,) * len(shape))
    nprev = len(prev)
    n_in = 14
    body = functools.partial(_mixer_body, nch=nch, chunk_seq=chunk_seq, tiles_per_seq=tiles_per_seq,
                             nprev=nprev)
    return pl.pallas_call(
        body,
        grid=(t // tm,),
        in_specs=[pl.BlockSpec((tm, d), lambda i: (i, 0)),
                  pl.BlockSpec((spt, 6, d), lambda i: (seq_of(i), 0, 0)),
                  pl.BlockSpec((spt, hh, hdk, hdv), lambda i: (seq_of(i), 0, 0, 0)),
                  pl.BlockSpec((spt, gh, gdk, gdv), lambda i: (seq_of(i), 0, 0, 0)),
                  const((1, d)), const((1, d)), const((d, nz)), const(p["lb_logits"].shape),
                  const(p["gk_up"].shape), const(p["gk_bias"].shape), const(p["hg_norm"].shape),
                  const(p["gla_norm"].shape), const((d, d)), const((ne, d))]
                 + [pl.BlockSpec(memory_space=pl.ANY)] * nprev,
        out_specs=[pl.BlockSpec((tm, d), lambda i: (i + tile_offset, 0)),
                   pl.BlockSpec((tm * NSLAB, LANE), lambda i: (i + tile_offset, 0)),
                   pl.BlockSpec((ne, tm), lambda i: (0, i + tile_offset)),
                   pl.BlockSpec((spt, hh, hdk, hdv), lambda i: (seq_of(i), 0, 0, 0)),
                   pl.BlockSpec((spt, gh, gdk, gdv), lambda i: (seq_of(i), 0, 0, 0))],
        out_shape=[jax.ShapeDtypeStruct((t_total, d), f32),
                   jax.ShapeDtypeStruct((t_total * NSLAB, LANE), f32),
                   jax.ShapeDtypeStruct((ne, t_total), f32),
                   jax.ShapeDtypeStruct(hg0.shape, f32),
                   jax.ShapeDtypeStruct(gl0.shape, f32)],
        input_output_aliases={n_in + j: j for j in range(nprev)},
        scratch_shapes=[pltpu.VMEM((tm, d), bf16),
                        pltpu.VMEM((tm, nz), f32),
                        pltpu.VMEM((tm, d), bf16),
                        pltpu.VMEM((tm, d), f32),
                        pltpu.VMEM((spt, hh, hdv, hdk), f32),
                        pltpu.VMEM((spt, gh, gdv, LANE), f32)],
        compiler_params=_cparams("arbitrary"),
        name="mixer_sample" if chunk_seq else "mixer_prompt",
    )(x.reshape(t, d), mod, hg0, gl0, p["norm1"], p["norm2"], p["w_in"], p["lb_logits"], p["gk_up"],
      p["gk_bias"], p["hg_norm"], p["gla_norm"], p["w_out"], p["w_rt"], *prev)


def _route_body(lg_ref, bias_ref, eidx_ref, wsel_ref, rank_ref, cnt_ref, run_s, s_s, cand_s):
    i = pl.program_id(0)
    ne, tt = lg_ref.shape
    gsz = ne // N_GROUPS
    ninf = -jnp.inf

    @pl.when(i == 0)
    def _():
        run_s[...] = jnp.zeros_like(run_s)

    row = lax.broadcasted_iota(i32, (ne, tt), 0)
    rg = lax.broadcasted_iota(i32, (gsz, tt), 0)

    gs = []
    for g in range(N_GROUPS):
        grows = pl.ds(g * gsz, gsz)
        sg = jax.nn.sigmoid(lg_ref[grows, :])
        s_s[grows, :] = sg
        blk = sg + bias_ref[grows, :]
        m1 = jnp.max(blk, axis=0, keepdims=True)
        first = jnp.min(jnp.where(blk == m1, rg, gsz), axis=0, keepdims=True)
        m2 = jnp.max(jnp.where(rg == first, ninf, blk), axis=0, keepdims=True)
        gs.append(m1 + m2)

    picked = [jnp.zeros((1, tt), f32) for _ in range(N_GROUPS)]
    for _ in range(TOPK_GROUPS):
        cur = [jnp.where(picked[g] > 0.5, ninf, gs[g]) for g in range(N_GROUPS)]
        m = functools.reduce(jnp.maximum, cur)
        gi = jnp.full((1, tt), N_GROUPS, i32)
        for g in reversed(range(N_GROUPS)):
            gi = jnp.where(cur[g] == m, g, gi)
        picked = [jnp.where(gi == g, 1.0, picked[g]) for g in range(N_GROUPS)]

    for g in range(N_GROUPS):
        grows = pl.ds(g * gsz, gsz)
        allowed = jnp.broadcast_to(picked[g], (gsz, tt)) > 0.5
        cand_s[grows, :] = jnp.where(allowed, s_s[grows, :] + bias_ref[grows, :], ninf)
    cand = cand_s[...]
    s = s_s[...]

    eis, ws = [], []
    chosen = jnp.zeros((ne, tt), jnp.bool_)
    for _ in range(TOP_K):
        m = jnp.max(cand, axis=0, keepdims=True)
        ei = jnp.min(jnp.where(cand == m, row, ne), axis=0, keepdims=True)
        hit = row == ei
        ws.append(jnp.sum(jnp.where(hit, s, 0.0), axis=0, keepdims=True))
        cand = jnp.where(hit, ninf, cand)
        chosen = chosen | hit
        eis.append(ei)
    wsum = functools.reduce(jnp.add, ws)
    scale = ROUTE_SCALE / wsum

    onehot = chosen.astype(bf16)
    ti = lax.broadcasted_iota(i32, (tt, tt), 0)
    tj = lax.broadcasted_iota(i32, (tt, tt), 1)
    before = (ti < tj).astype(bf16)
    prior = jnp.dot(onehot, before, preferred_element_type=f32) + run_s[:, 0:1]
    for k in range(TOP_K):
        hit = row == eis[k]
        eidx_ref[k:k + 1, :] = eis[k]
        wsel_ref[k:k + 1, :] = ws[k] * scale
        rank_ref[k:k + 1, :] = jnp.sum(jnp.where(hit, prior, 0.0), axis=0, keepdims=True).astype(i32)
    run_s[...] = run_s[...] + jnp.sum(chosen.astype(f32), axis=1, keepdims=True)
    cnt_ref[...] = run_s[...].astype(i32)


def _route(lgt, b_router):
    ne, t = lgt.shape
    tt = ROUTE_TILE
    tok = lambda dt: jax.ShapeDtypeStruct((TOP_K, t), dt)
    return pl.pallas_call(
        _route_body,
        grid=(t // tt,),
        in_specs=[pl.BlockSpec((ne, tt), lambda i: (0, i)),
                  pl.BlockSpec((ne, 1), lambda i: (0, 0))],
        out_specs=[pl.BlockSpec((TOP_K, tt), lambda i: (0, i)),
                   pl.BlockSpec((TOP_K, tt), lambda i: (0, i)),
                   pl.BlockSpec((TOP_K, tt), lambda i: (0, i)),
                   pl.BlockSpec((ne, LANE), lambda i: (0, 0))],
        out_shape=[tok(i32), tok(f32), tok(i32), jax.ShapeDtypeStruct((ne, LANE), i32)],
        scratch_shapes=[pltpu.VMEM((ne, LANE), f32), pltpu.VMEM((ne, tt), f32), pltpu.VMEM((ne, tt), f32)],
        compiler_params=_cparams("arbitrary"),
        name="route",
    )(lgt, b_router.reshape(ne, 1))


def _dest_body(eidx_ref, rank_ref, pstart_ref, dest_ref):
    ne = pstart_ref.shape[0]
    tt = eidx_ref.shape[1]
    row = lax.broadcasted_iota(i32, (ne, tt), 0)
    ps = pstart_ref[...]
    for k in range(TOP_K):
        hit = row == eidx_ref[k:k + 1, :]
        base = jnp.sum(jnp.where(hit, ps, 0.0), axis=0, keepdims=True)
        dest_ref[0, k:k + 1, :] = base.astype(i32) + rank_ref[k:k + 1, :]


def _dest(eidx_t, rank_t, pstarts):
    _, t = eidx_t.shape
    ne = pstarts.shape[0]
    tt = COMB_TILE
    return pl.pallas_call(
        _dest_body,
        grid=(t // tt,),
        in_specs=[pl.BlockSpec((TOP_K, tt), lambda i: (0, i)),
                  pl.BlockSpec((TOP_K, tt), lambda i: (0, i)),
                  pl.BlockSpec((ne, 1), lambda i: (0, 0))],
        out_specs=pl.BlockSpec((1, TOP_K, tt), lambda i: (i, 0, 0)),
        out_shape=jax.ShapeDtypeStruct((t // tt, TOP_K, tt), i32),
        compiler_params=_cparams("arbitrary"),
        name="dest",
    )(eidx_t, rank_t, pstarts.astype(f32).reshape(ne, 1))


def _dispatch_body(dest_ref, padstart_ref, hf_ref, xs_ref, zero_s, sem, zsem):
    i = pl.program_id(0)
    tt = hf_ref.shape[0] // NSLAB
    ct = dest_ref.shape[2]
    ne = padstart_ref.shape[0]
    nfill = zero_s.shape[0] // NSLAB

    @pl.when(i == 0)
    def _():
        zero_s[...] = jnp.zeros_like(zero_s)

        def fill(e, c):
            pltpu.make_async_copy(zero_s, _slab(xs_ref, padstart_ref[e], nfill), zsem).start()
            return c

        lax.fori_loop(0, ne, fill, 0)

        def drain(e, c):
            pltpu.make_async_copy(zero_s, _slab(xs_ref, 0, nfill), zsem).wait()
            return c

        lax.fori_loop(0, ne, drain, 0)

    for part in range(tt // ct):
        def send(t, c, part=part):
            for k in range(TOP_K):
                pltpu.make_async_copy(_slab(hf_ref, part * ct + t, 1),
                                      _slab(xs_ref, dest_ref[part, k, t], 1), sem).start()
            return c

        lax.fori_loop(0, ct, send, 0)
    for k in range(TOP_K):
        pltpu.make_async_copy(hf_ref, _slab(xs_ref, 0, tt), sem).wait()


def _dispatch(dest3, padstart, hfs, p_rows):
    t = hfs.shape[0] // NSLAB
    tt = ROUTE_TILE
    parts = tt // dest3.shape[2]
    return pl.pallas_call(
        _dispatch_body,
        grid=(t // tt,),
        in_specs=[pl.BlockSpec((parts, TOP_K, dest3.shape[2]), lambda i: (i, 0, 0), memory_space=pltpu.SMEM),
                  pl.BlockSpec(memory_space=pltpu.SMEM),
                  pl.BlockSpec((tt * NSLAB, LANE), lambda i: (i, 0))],
        out_specs=pl.BlockSpec(memory_space=pl.ANY),
        out_shape=jax.ShapeDtypeStruct(((p_rows + MOE_BLOCK) * NSLAB, LANE), f32),
        scratch_shapes=[pltpu.VMEM((MOE_BLOCK * NSLAB, LANE), f32), pltpu.SemaphoreType.DMA,
                        pltpu.SemaphoreType.DMA],
        compiler_params=_cparams("arbitrary"),
        name="dispatch",
    )(dest3, padstart, hfs)


def _experts_body(bstart_ref, nblk_ref, nused_ref, xs_ref, wg_ref, wu_ref, wd_ref, eo_ref,
                  xbuf, obuf, wgu_s, wd_s, isem, osem):
    e = pl.program_id(0)
    blk = xbuf.shape[0] // (2 * NSLAB)
    f = wg_ref.shape[2]
    nused = nused_ref[0]

    def in_copy(g, slot):
        return pltpu.make_async_copy(_slab(xs_ref, g * blk, blk), _slab(xbuf, slot * blk, blk), isem.at[slot])

    def out_copy(g, slot):
        return pltpu.make_async_copy(_slab(obuf, slot * blk, blk), _slab(eo_ref, g * blk, blk), osem.at[slot])

    @pl.when(e == 0)
    def _():
        in_copy(0, 0).start()

    @pl.when(nblk_ref[e] > 0)
    def _():
        wgu_s[:, 0:f] = wg_ref[0].astype(bf16)
        wgu_s[:, f:2 * f] = wu_ref[0].astype(bf16)
        wd_s[...] = wd_ref[0].astype(bf16)

        def block(g, c):
            slot = g & 1
            in_copy(g, slot).wait()

            @pl.when(g + 1 < nused)
            def _():
                in_copy(g + 1, 1 - slot).start()

            x = _load_rows(xbuf, slot * blk, blk).astype(bf16)
            gu = jnp.dot(x, wgu_s[...], preferred_element_type=f32)
            gate = gu[:, 0:f]
            h = (gate * jax.nn.sigmoid(gate) * gu[:, f:2 * f]).astype(bf16)
            o = jnp.dot(h, wd_s[...], preferred_element_type=f32)

            @pl.when(g >= 2)
            def _():
                out_copy(g - 2, slot).wait()

            _store_rows(obuf, slot * blk, o)
            out_copy(g, slot).start()
            return c

        lax.fori_loop(bstart_ref[e], bstart_ref[e] + nblk_ref[e], block, 0)

    @pl.when(e == pl.num_programs(0) - 1)
    def _():
        out_copy(nused - 1, (nused - 1) & 1).wait()

        @pl.when(nused >= 2)
        def _():
            out_copy(nused - 2, nused & 1).wait()


def _experts(bstart, nblk, nused, xs, w_gate, w_up, w_down, nb):
    ne, d, f = w_gate.shape
    assert d == NSLAB * LANE
    blk = MOE_BLOCK
    w_map = lambda e, *_: (e, 0, 0)
    grid_spec = pltpu.PrefetchScalarGridSpec(
        num_scalar_prefetch=3,
        grid=(ne,),
        in_specs=[pl.BlockSpec(memory_space=pl.ANY),
                  pl.BlockSpec((1, d, f), w_map),
                  pl.BlockSpec((1, d, f), w_map),
                  pl.BlockSpec((1, f, d), w_map)],
        out_specs=pl.BlockSpec(memory_space=pl.ANY),
        scratch_shapes=[pltpu.VMEM((2 * blk * NSLAB, LANE), f32), pltpu.VMEM((2 * blk * NSLAB, LANE), f32),
                        pltpu.VMEM((d, 2 * f), bf16), pltpu.VMEM((f, d), bf16),
                        pltpu.SemaphoreType.DMA((2,)), pltpu.SemaphoreType.DMA((2,))],
    )
    return pl.pallas_call(
        _experts_body,
        grid_spec=grid_spec,
        out_shape=jax.ShapeDtypeStruct((nb * blk * NSLAB, LANE), f32),
        compiler_params=_cparams("arbitrary"),
        name="experts",
    )(bstart, nblk, nused, xs, w_gate, w_up, w_down)


def _combine_body(dest_ref, eo_ref, x1_ref, hp_ref, w_ref, mod_ref, wsg_ref, wsu_ref, wsd_ref, nf_ref,
                  y_ref, buf, sem, *, chunk_seq):
    tt, d = x1_ref.shape

    def fetch(t, c):
        for k in range(TOP_K):
            pltpu.make_async_copy(_slab(eo_ref, dest_ref[0, k, t], 1), _slab(buf, k * tt + t, 1), sem).start()
        return c

    lax.fori_loop(0, tt, fetch, 0)

    hf = _load_rows(hp_ref, 0, tt).astype(bf16)
    g = jnp.dot(hf, wsg_ref[...], preferred_element_type=f32)
    u = jnp.dot(hf, wsu_ref[...], preferred_element_type=f32)
    acc = jnp.dot((g * jax.nn.sigmoid(g) * u).astype(bf16), wsd_ref[...], preferred_element_type=f32)

    for k in range(TOP_K):
        pltpu.make_async_copy(_slab(eo_ref, 0, tt), _slab(buf, k * tt, tt), sem).wait()
    for k in range(TOP_K):
        acc = acc + _load_rows(buf, k * tt, tt) * w_ref[:, k:k + 1]

    for j in range(tt // CHUNK):
        rows = pl.ds(j * CHUNK, CHUNK)
        g2 = mod_ref[j if chunk_seq else 0][5:6, :]
        y = x1_ref[rows, :] + g2 * acc[j * CHUNK:(j + 1) * CHUNK, :]
        y_ref[rows, :] = _rms(y) * nf_ref[...]


def _combine(dest3, eo, x1, hp, wsel, mod, p, nseq, seq_len, tile_offset):
    d = x1.shape[1]
    tt = COMB_TILE
    f = p["ws_gate"].shape[1]
    t = nseq * seq_len
    chunk_seq = seq_len == CHUNK
    if chunk_seq:
        spt, tiles_per_seq = tt // CHUNK, 1
        seq_of = lambda i: i
    else:
        assert seq_len % tt == 0
        spt, tiles_per_seq = 1, seq_len // tt
        seq_of = lambda i: i // tiles_per_seq
    body = functools.partial(_combine_body, chunk_seq=chunk_seq)
    const = lambda shape: pl.BlockSpec(shape, lambda i: (0,) * len(shape))
    tok = lambda w: pl.BlockSpec((tt, w), lambda i: (i + tile_offset, 0))
    return pl.pallas_call(
        body,
        grid=(t // tt,),
        in_specs=[pl.BlockSpec((1, TOP_K, tt), lambda i: (i + tile_offset, 0, 0), memory_space=pltpu.SMEM),
                  pl.BlockSpec(memory_space=pl.ANY),
                  tok(d), pl.BlockSpec((tt * NSLAB, LANE), lambda i: (i + tile_offset, 0)), tok(TOP_K),
                  pl.BlockSpec((spt, 6, d), lambda i: (seq_of(i), 0, 0)),
                  const((d, f)), const((d, f)), const((f, d)), const((1, d))],
        out_specs=pl.BlockSpec((tt, d), lambda i: (i, 0)),
        out_shape=jax.ShapeDtypeStruct((t, d), f32),
        scratch_shapes=[pltpu.VMEM((TOP_K * tt * NSLAB, LANE), f32), pltpu.SemaphoreType.DMA],
        compiler_params=_cparams("arbitrary"),
        name="combine_sample" if chunk_seq else "combine_prompt",
    )(dest3, eo, x1, hp, wsel, mod, p["ws_gate"], p["ws_up"], p["ws_down"], p["norm_final"])


def kernel(x_prompt, x_sample, state_hgrn, state_gla, c_prompt, c_sample, w_ada, b_ada, norm1, norm2, w_in,
           hg_lb_logits, gla_gk_up, gla_gk_bias, hg_out_norm, gla_out_norm, w_out, w_router, b_router,
           w_gate, w_up, w_down, ws_gate, ws_up, ws_down, norm_final):
    nbp, lp, d = x_prompt.shape
    nbs, ls, _ = x_sample.shape
    depth = w_in.shape[0]
    assert depth == 1
    ne = w_router.shape[2]
    hgw = hg_out_norm.shape[1]
    gkw = gla_gk_up.shape[2]
    rank = gla_gk_up.shape[1]
    d_in = w_in.shape[2]

    nz = d_in - rank + LANE
    w_in_p = jnp.pad(w_in[0].astype(bf16), ((0, 0), (0, nz - d_in)))
    p = dict(
        norm1=norm1[0].reshape(1, d), norm2=norm2[0].reshape(1, d), w_in=w_in_p,
        lb_logits=hg_lb_logits,
        gk_up=jnp.zeros((LANE, gkw), f32).at[:rank].set(gla_gk_up[0]),
        gk_bias=gla_gk_bias[0].reshape(1, gkw),
        hg_norm=hg_out_norm[0].reshape(1, hgw), gla_norm=gla_out_norm[0].reshape(1, -1),
        w_out=w_out[0].astype(bf16), w_rt=w_router[0].T,
        ws_gate=ws_gate[0].astype(bf16), ws_up=ws_up[0].astype(bf16), ws_down=ws_down[0].astype(bf16),
        norm_final=norm_final.reshape(1, d),
    )

    c_all = jnp.concatenate([c_prompt, c_sample], axis=0)
    mod = _ada(c_all, w_ada[0], b_ada[0]).reshape(nbp + nbs, 6, d)

    zero_hg = jnp.zeros((nbp,) + state_hgrn.shape[2:], f32)
    zero_gl = jnp.zeros((nbp,) + state_gla.shape[2:], f32)
    tp, ts = nbp * lp, nbs * ls
    t = tp + ts
    x1, hp, lgt, hg_p, gl_p = _mixer(x_prompt, mod[:nbp], zero_hg, zero_gl, p, False, t, 0)
    x1, hp, lgt, hg_s, gl_s = _mixer(x_sample, mod[nbp:], state_hgrn[0], state_gla[0], p, True, t,
                                     tp // MIX_TILE, prev=(x1, hp, lgt))

    eidx_t, wsel_t, rank_t, cnt = _route(lgt, b_router[0])

    counts = cnt[:, 0]
    padded = (counts + MOE_BLOCK - 1) // MOE_BLOCK * MOE_BLOCK
    pends = jnp.cumsum(padded)
    pstarts = pends - padded
    nb = -(-(t * TOP_K + ne * (MOE_BLOCK - 1)) // MOE_BLOCK)
    nused = (pends[-1] // MOE_BLOCK).astype(i32).reshape(1)

    dest3 = _dest(eidx_t, rank_t, pstarts)
    xs = _dispatch(dest3, (pstarts + counts).astype(i32), hp, nb * MOE_BLOCK)
    eo = _experts((pstarts // MOE_BLOCK).astype(i32), (padded // MOE_BLOCK).astype(i32), nused, xs,
                  w_gate[0], w_up[0], w_down[0], nb)
    wsel = wsel_t.T
    y_prompt = _combine(dest3, eo, x1, hp, wsel, mod[:nbp], p, nbp, lp, 0)
    y_sample = _combine(dest3, eo, x1, hp, wsel, mod[nbp:], p, nbs, ls, tp // COMB_TILE)
    return (y_prompt.reshape(nbp, lp, d), y_sample.reshape(nbs, ls, d),
            hg_p[None], gl_p[None], hg_s[None], gl_s[None])
```

```python
import functools

import jax
import jax.numpy as jnp
from jax import lax
from jax.experimental import pallas as pl
from jax.experimental.pallas import tpu as pltpu

f32 = jnp.float32
bf16 = jnp.bfloat16
i32 = jnp.int32
HIGHEST = lax.Precision.HIGHEST

EPS = 1e-6
CHUNK = 64
HG_HEADS = 4
GLA_HEADS = 4
GLA_GATE_NORMALIZER = 16.0
TOP_K = 8
N_GROUPS = 8
TOPK_GROUPS = 4
ROUTE_SCALE = 2.5

LANE = 128
SUBLANE = 8
V7X_VMEM_BYTES = 64 * 1024 * 1024
VMEM_LIMIT = V7X_VMEM_BYTES - 8 * 1024 * 1024

MIX_TILE = 512
ROUTE_TILE = 512
COMB_TILE = 256
MOE_BLOCK = 256
IN_SLOTS = 3


def _cparams(*sem):
    return pltpu.CompilerParams(dimension_semantics=sem, vmem_limit_bytes=VMEM_LIMIT)


def _sigmoid_pair(x):
    e = jnp.exp(-jnp.abs(x))
    r = 1.0 / (1.0 + e)
    er = e * r
    pos = x >= 0
    return jnp.where(pos, r, er), jnp.where(pos, er, r)


def _rms(x):
    return x * lax.rsqrt(jnp.mean(x * x, axis=-1, keepdims=True) + EPS)


NSLAB = SUBLANE


def _slab(ref, row0, n):
    return ref.at[pl.ds(pl.multiple_of(row0 * NSLAB, NSLAB), n * NSLAB), :]


def _load_rows(ref, row0, n):
    return jnp.concatenate([ref[pl.ds(row0 * NSLAB + s, n, stride=NSLAB), :] for s in range(NSLAB)], axis=1)


def _store_rows(ref, row0, x):
    for s in range(NSLAB):
        ref[pl.ds(row0 * NSLAB + s, x.shape[0], stride=NSLAB), :] = x[:, s * LANE:(s + 1) * LANE]


def _ada_body(c_ref, w_ref, b_ref, o_ref):
    c = c_ref[...]
    a = c * jax.nn.sigmoid(c)
    o_ref[...] = jnp.dot(a, w_ref[...], preferred_element_type=f32, precision=HIGHEST) + b_ref[...]


def _ada(c_all, w, b):
    ns, d = c_all.shape
    n = w.shape[1]
    tn = 1024
    return pl.pallas_call(
        _ada_body,
        grid=(n // tn,),
        in_specs=[pl.BlockSpec((ns, d), lambda j: (0, 0)),
                  pl.BlockSpec((d, tn), lambda j: (0, j)),
                  pl.BlockSpec((1, tn), lambda j: (0, j))],
        out_specs=pl.BlockSpec((ns, tn), lambda j: (0, j)),
        out_shape=jax.ShapeDtypeStruct((ns, n), f32),
        compiler_params=_cparams("arbitrary"),
        name="ada",
    )(c_all, w, b.reshape(1, n))


LEVELS = tuple(1 << n for n in range(CHUNK.bit_length() - 1))
NT = (((1,), (1,)), ((), ()))
TN = (((0,), (0,)), ((), ()))


def _pair_level(c):
    t = lax.broadcasted_iota(i32, (c, c), 0)
    s = lax.broadcasted_iota(i32, (c, c), 1)
    x = t ^ s
    lvl = jnp.full((c, c), -1, i32)
    for li, m in enumerate(LEVELS):
        lvl = jnp.where(x >= m, li, lvl)
    return jnp.where(t < s, -2, lvl)


def _chunk_scan(q, k, g):
    c = q.shape[0]
    row = lax.broadcasted_iota(i32, q.shape, 0)
    pre = g
    tot = g
    zs = []
    for m in LEVELS:
        upper = (row & (2 * m - 1)) >= m
        e = jnp.exp(jnp.where(upper, pre, tot - pre))
        zs.append((jnp.where(upper, q, k) * e).astype(bf16))
        below = pltpu.roll(tot, m, 0)
        above = pltpu.roll(tot, c - m, 0)
        pre = pre + jnp.where(upper, below, 0.0)
        tot = tot + jnp.where(upper, below, above)
    qb = (q * jnp.exp(pre)).astype(bf16)
    kd = (k * jnp.exp(tot - pre)).astype(bf16)
    return zs, qb, kd, tot[0:1, :]


def _chunk_head(zs, qd, kd0, qb, kd, decay, v, st_ref, slot, head, keep, lvl):
    own = (lambda a: a) if keep is None else (lambda a: a * keep)
    att = jnp.where(lvl == -1, lax.dot_general(own(qd), kd0, NT, preferred_element_type=f32), 0.0)
    for li, z in enumerate(zs):
        att = att + jnp.where(lvl == li, lax.dot_general(own(z), z, NT, preferred_element_type=f32), 0.0)
    st = st_ref[slot, head]
    o = jnp.dot(att.astype(bf16), v, preferred_element_type=f32)
    o = o + lax.dot_general(own(qb), st.astype(bf16), NT, preferred_element_type=f32)
    st_ref[slot, head] = st * decay + lax.dot_general(v, own(kd), TN, preferred_element_type=f32)
    return o


def _mixer_body(*refs, nch, chunk_seq, tiles_per_seq, nprev):
    (x_ref, mod_ref, hg0_ref, gl0_ref, n1_ref, n2_ref, win_ref, lbl_ref, gup_ref, gbias_ref,
     hgn_ref, gln_ref, wout_ref, wrt_ref) = refs[:14]
    (x1_ref, hp_ref, lgt_ref, hg_ref, gl_ref, h_s, z_s, o_s, hf_s, slab_s, hgt_s, glt_s) = refs[14 + nprev:]
    i = pl.program_id(0)
    d = x_ref.shape[1]
    hdk, hdv = hg0_ref.shape[2], hg0_ref.shape[3]
    gdk, gdv = gl0_ref.shape[2], gl0_ref.shape[3]
    assert hdk == LANE and LANE % gdk == 0
    hgw = HG_HEADS * hdk
    gkw = GLA_HEADS * gdk
    gvw = GLA_HEADS * gdv
    nslot = hgt_s.shape[0]
    gla_lanes = lambda h: pl.ds((h * gdk) % LANE, gdk)

    def load_state():
        for s in range(nslot):
            for h in range(HG_HEADS):
                hgt_s[s, h] = hg0_ref[s, h].T
            for h in range(GLA_HEADS):
                glt_s[s, h] = jnp.zeros(glt_s.shape[2:], f32)
                glt_s[s, h, :, gla_lanes(h)] = gl0_ref[s, h].T

    if chunk_seq:
        load_state()
    else:
        pl.when(i % tiles_per_seq == 0)(load_state)

    lbl = lbl_ref[...]
    lbe = jnp.exp(lbl - jnp.max(lbl, axis=0, keepdims=True))
    lb = lbe[0:1, :] / jnp.sum(lbe, axis=0, keepdims=True)

    for j in range(nch):
        rows = pl.ds(j * CHUNK, CHUNK)
        m = mod_ref[j if chunk_seq else 0]
        xn = _rms(x_ref[rows, :]) * n1_ref[...]
        h_s[rows, :] = (xn * (1.0 + m[1:2, :]) + m[0:1, :]).astype(bf16)

    nz = z_s.shape[1]
    step = 512
    for n0 in range(0, nz, step):
        n1 = min(n0 + step, nz)
        z_s[:, n0:n1] = jnp.dot(h_s[...], win_ref[:, n0:n1], preferred_element_type=f32)

    o_hq, o_hf, o_hi, o_hgate = 0, hgw, 2 * hgw, 3 * hgw
    o_gq = 4 * hgw
    o_gk = o_gq + gkw
    o_gv = o_gk + gkw
    o_ggate = o_gv + gvw
    o_glr = o_ggate + gvw

    lvl = _pair_level(CHUNK)
    lane = lax.broadcasted_iota(i32, (CHUNK, LANE), 1)
    heads_per_block = LANE // gdk
    for j in range(nch):
        rows = pl.ds(j * CHUNK, CHUNK)
        slot = j if chunk_seq else 0
        for h in range(HG_HEADS):
            lbh = lb[:, h * hdk:(h + 1) * hdk]
            hq = z_s[rows, pl.ds(o_hq + h * hdk, hdk)]
            sig, nsig = _sigmoid_pair(z_s[rows, pl.ds(o_hf + h * hdk, hdk)])
            g = jnp.log(lbh + (1.0 - lbh) * sig)
            k = (1.0 - lbh) * nsig
            q = hq * jax.nn.sigmoid(hq) * (hdk ** -0.5)
            zs, qb, kd, b_last = _chunk_scan(q, k, g)
            v = z_s[rows, pl.ds(o_hi + h * hdv, hdv)].astype(bf16)
            o = _chunk_head(zs, q.astype(bf16), k.astype(bf16), qb, kd, jnp.exp(b_last), v,
                            hgt_s, slot, h, None, lvl)
            hgate = z_s[rows, pl.ds(o_hgate + h * hdv, hdv)]
            o = _rms(o) * hgn_ref[:, pl.ds(h * hdv, hdv)] * jax.nn.sigmoid(hgate)
            o_s[rows, pl.ds(h * hdv, hdv)] = o.astype(bf16)
        glr = z_s[rows, pl.ds(o_glr, LANE)]
        u = jnp.dot(glr, gup_ref[...], preferred_element_type=f32, precision=HIGHEST) + gbias_ref[...]
        loga = (jnp.minimum(u, 0.0) - jnp.log1p(jnp.exp(-jnp.abs(u)))) * (1.0 / GLA_GATE_NORMALIZER)
        for blk in range(GLA_HEADS // heads_per_block):
            q = z_s[rows, pl.ds(o_gq + blk * LANE, LANE)] * (gdk ** -0.5)
            k = z_s[rows, pl.ds(o_gk + blk * LANE, LANE)]
            zs, qb, kd, b_last = _chunk_scan(q, k, loga[:, blk * LANE:(blk + 1) * LANE])
            qd, kd0, decay = q.astype(bf16), k.astype(bf16), jnp.exp(b_last)
            for h in range(blk * heads_per_block, (blk + 1) * heads_per_block):
                lo = (h * gdk) % LANE
                keep = ((lane >= lo) & (lane < lo + gdk)).astype(bf16)
                v = z_s[rows, pl.ds(o_gv + h * gdv, gdv)].astype(bf16)
                o = _chunk_head(zs, qd, kd0, qb, kd, decay, v, glt_s, slot, h, keep, lvl)
                ggate = z_s[rows, pl.ds(o_ggate + h * gdv, gdv)]
                o = _rms(o) * gln_ref[...] * (ggate * jax.nn.sigmoid(ggate))
                o_s[rows, pl.ds(hgw + h * gdv, gdv)] = o.astype(bf16)

    a = jnp.dot(o_s[...], wout_ref[...], preferred_element_type=f32)
    for j in range(nch):
        rows = pl.ds(j * CHUNK, CHUNK)
        m = mod_ref[j if chunk_seq else 0]
        x1 = x_ref[rows, :] + m[2:3, :] * a[j * CHUNK:(j + 1) * CHUNK, :]
        x1_ref[rows, :] = x1
        hf = _rms(x1) * n2_ref[...] * (1.0 + m[4:5, :]) + m[3:4, :]
        hf_s[rows, :] = hf
        _store_rows(slab_s, j * CHUNK, hf)
    hp_ref[...] = slab_s[...].astype(bf16)
    lgt_ref[...] = lax.dot_general(wrt_ref[...], hf_s[...], (((1,), (1,)), ((), ())),
                                   preferred_element_type=f32, precision=HIGHEST)

    def store_state():
        for s in range(nslot):
            for h in range(HG_HEADS):
                hg_ref[s, h] = hgt_s[s, h].T
            for h in range(GLA_HEADS):
                gl_ref[s, h] = glt_s[s, h, :, gla_lanes(h)].T

    if chunk_seq:
        store_state()
    else:
        pl.when(i % tiles_per_seq == tiles_per_seq - 1)(store_state)


def _mixer(x, mod, hg0, gl0, p, chunk_seq, t_total, tile_offset, prev=()):
    nseq, L, d = x.shape
    t = nseq * L
    tm = MIX_TILE
    nch = tm // CHUNK
    if chunk_seq:
        assert L == CHUNK and nseq % nch == 0
        spt, tiles_per_seq = nch, 1
        seq_of = lambda i: i
    else:
        assert L % tm == 0
        spt, tiles_per_seq = 1, L // tm
        seq_of = lambda i: i // tiles_per_seq
    ne = p["w_rt"].shape[0]
    nz = p["w_in"].shape[1]
    _, hh, hdk, hdv = hg0.shape
    _, gh, gdk, gdv = gl0.shape
    const = lambda shape: pl.BlockSpec(shape, lambda i: (0,) * len(shape))
    nprev = len(prev)
    n_in = 14
    body = functools.partial(_mixer_body, nch=nch, chunk_seq=chunk_seq, tiles_per_seq=tiles_per_seq,
                             nprev=nprev)
    return pl.pallas_call(
        body,
        grid=(t // tm,),
        in_specs=[pl.BlockSpec((tm, d), lambda i: (i, 0)),
                  pl.BlockSpec((spt, 6, d), lambda i: (seq_of(i), 0, 0)),
                  pl.BlockSpec((spt, hh, hdk, hdv), lambda i: (seq_of(i), 0, 0, 0)),
                  pl.BlockSpec((spt, gh, gdk, gdv), lambda i: (seq_of(i), 0, 0, 0)),
                  const((1, d)), const((1, d)), const((d, nz)), const(p["lb_logits"].shape),
                  const(p["gk_up"].shape), const(p["gk_bias"].shape), const(p["hg_norm"].shape),
                  const(p["gla_norm"].shape), const((d, d)), const((ne, d))]
                 + [pl.BlockSpec(memory_space=pl.ANY)] * nprev,
        out_specs=[pl.BlockSpec((tm, d), lambda i: (i + tile_offset, 0)),
                   pl.BlockSpec((tm * NSLAB, LANE), lambda i: (i + tile_offset, 0)),
                   pl.BlockSpec((ne, tm), lambda i: (0, i + tile_offset)),
                   pl.BlockSpec((spt, hh, hdk, hdv), lambda i: (seq_of(i), 0, 0, 0)),
                   pl.BlockSpec((spt, gh, gdk, gdv), lambda i: (seq_of(i), 0, 0, 0))],
        out_shape=[jax.ShapeDtypeStruct((t_total, d), f32),
                   jax.ShapeDtypeStruct((t_total * NSLAB, LANE), bf16),
                   jax.ShapeDtypeStruct((ne, t_total), f32),
                   jax.ShapeDtypeStruct(hg0.shape, f32),
                   jax.ShapeDtypeStruct(gl0.shape, f32)],
        input_output_aliases={n_in + j: j for j in range(nprev)},
        scratch_shapes=[pltpu.VMEM((tm, d), bf16),
                        pltpu.VMEM((tm, nz), f32),
                        pltpu.VMEM((tm, d), bf16),
                        pltpu.VMEM((tm, d), f32),
                        pltpu.VMEM((tm * NSLAB, LANE), f32),
                        pltpu.VMEM((spt, hh, hdv, hdk), f32),
                        pltpu.VMEM((spt, gh, gdv, LANE), f32)],
        compiler_params=_cparams("arbitrary"),
        name="mixer_sample" if chunk_seq else "mixer_prompt",
    )(x.reshape(t, d), mod, hg0, gl0, p["norm1"], p["norm2"], p["w_in"], p["lb_logits"], p["gk_up"],
      p["gk_bias"], p["hg_norm"], p["gla_norm"], p["w_out"], p["w_rt"], *prev)


def _route_body(lg_ref, bias_ref, eidx_ref, wsel_ref, rank_ref, cnt_ref, run_s, s_s, cand_s):
    i = pl.program_id(0)
    ne, tt = lg_ref.shape
    gsz = ne // N_GROUPS
    ninf = -jnp.inf

    @pl.when(i == 0)
    def _():
        run_s[...] = jnp.zeros_like(run_s)

    row = lax.broadcasted_iota(i32, (ne, tt), 0)
    rg = lax.broadcasted_iota(i32, (gsz, tt), 0)

    gs = []
    for g in range(N_GROUPS):
        grows = pl.ds(g * gsz, gsz)
        sg = jax.nn.sigmoid(lg_ref[grows, :])
        s_s[grows, :] = sg
        blk = sg + bias_ref[grows, :]
        m1 = jnp.max(blk, axis=0, keepdims=True)
        first = jnp.min(jnp.where(blk == m1, rg, gsz), axis=0, keepdims=True)
        m2 = jnp.max(jnp.where(rg == first, ninf, blk), axis=0, keepdims=True)
        gs.append(m1 + m2)

    picked = [jnp.zeros((1, tt), f32) for _ in range(N_GROUPS)]
    for _ in range(TOPK_GROUPS):
        cur = [jnp.where(picked[g] > 0.5, ninf, gs[g]) for g in range(N_GROUPS)]
        m = functools.reduce(jnp.maximum, cur)
        gi = jnp.full((1, tt), N_GROUPS, i32)
        for g in reversed(range(N_GROUPS)):
            gi = jnp.where(cur[g] == m, g, gi)
        picked = [jnp.where(gi == g, 1.0, picked[g]) for g in range(N_GROUPS)]

    for g in range(N_GROUPS):
        grows = pl.ds(g * gsz, gsz)
        allowed = jnp.broadcast_to(picked[g], (gsz, tt)) > 0.5
        cand_s[grows, :] = jnp.where(allowed, s_s[grows, :] + bias_ref[grows, :], ninf)
    cand = cand_s[...]
    s = s_s[...]

    eis, ws = [], []
    chosen = jnp.zeros((ne, tt), jnp.bool_)
    for _ in range(TOP_K):
        m = jnp.max(cand, axis=0, keepdims=True)
        ei = jnp.min(jnp.where(cand == m, row, ne), axis=0, keepdims=True)
        hit = row == ei
        ws.append(jnp.sum(jnp.where(hit, s, 0.0), axis=0, keepdims=True))
        cand = jnp.where(hit, ninf, cand)
        chosen = chosen | hit
        eis.append(ei)
    wsum = functools.reduce(jnp.add, ws)
    scale = ROUTE_SCALE / wsum

    onehot = chosen.astype(bf16)
    ti = lax.broadcasted_iota(i32, (tt, tt), 0)
    tj = lax.broadcasted_iota(i32, (tt, tt), 1)
    before = (ti < tj).astype(bf16)
    prior = jnp.dot(onehot, before, preferred_element_type=f32) + run_s[:, 0:1]
    for k in range(TOP_K):
        hit = row == eis[k]
        eidx_ref[k:k + 1, :] = eis[k]
        wsel_ref[k:k + 1, :] = ws[k] * scale
        rank_ref[k:k + 1, :] = jnp.sum(jnp.where(hit, prior, 0.0), axis=0, keepdims=True).astype(i32)
    run_s[...] = run_s[...] + jnp.sum(chosen.astype(f32), axis=1, keepdims=True)
    cnt_ref[...] = run_s[...].astype(i32)


def _route(lgt, b_router):
    ne, t = lgt.shape
    tt = ROUTE_TILE
    tok = lambda dt: jax.ShapeDtypeStruct((TOP_K, t), dt)
    return pl.pallas_call(
        _route_body,
        grid=(t // tt,),
        in_specs=[pl.BlockSpec((ne, tt), lambda i: (0, i)),
                  pl.BlockSpec((ne, 1), lambda i: (0, 0))],
        out_specs=[pl.BlockSpec((TOP_K, tt), lambda i: (0, i)),
                   pl.BlockSpec((TOP_K, tt), lambda i: (0, i)),
                   pl.BlockSpec((TOP_K, tt), lambda i: (0, i)),
                   pl.BlockSpec((ne, LANE), lambda i: (0, 0))],
        out_shape=[tok(i32), tok(f32), tok(i32), jax.ShapeDtypeStruct((ne, LANE), i32)],
        scratch_shapes=[pltpu.VMEM((ne, LANE), f32), pltpu.VMEM((ne, tt), f32), pltpu.VMEM((ne, tt), f32)],
        compiler_params=_cparams("arbitrary"),
        name="route",
    )(lgt, b_router.reshape(ne, 1))


def _dest_body(eidx_ref, rank_ref, pstart_ref, dest_ref):
    ne = pstart_ref.shape[0]
    tt = eidx_ref.shape[1]
    row = lax.broadcasted_iota(i32, (ne, tt), 0)
    ps = pstart_ref[...]
    for k in range(TOP_K):
        hit = row == eidx_ref[k:k + 1, :]
        base = jnp.sum(jnp.where(hit, ps, 0.0), axis=0, keepdims=True)
        dest_ref[0, k:k + 1, :] = base.astype(i32) + rank_ref[k:k + 1, :]


def _dest(eidx_t, rank_t, pstarts):
    _, t = eidx_t.shape
    ne = pstarts.shape[0]
    tt = COMB_TILE
    return pl.pallas_call(
        _dest_body,
        grid=(t // tt,),
        in_specs=[pl.BlockSpec((TOP_K, tt), lambda i: (0, i)),
                  pl.BlockSpec((TOP_K, tt), lambda i: (0, i)),
                  pl.BlockSpec((ne, 1), lambda i: (0, 0))],
        out_specs=pl.BlockSpec((1, TOP_K, tt), lambda i: (i, 0, 0)),
        out_shape=jax.ShapeDtypeStruct((t // tt, TOP_K, tt), i32),
        compiler_params=_cparams("arbitrary"),
        name="dest",
    )(eidx_t, rank_t, pstarts.astype(f32).reshape(ne, 1))


def _dispatch_body(dest_ref, padstart_ref, hf_ref, xs_ref, zero_s, sem, zsem):
    i = pl.program_id(0)
    tt = hf_ref.shape[0] // NSLAB
    ct = dest_ref.shape[2]
    ne = padstart_ref.shape[0]
    nfill = zero_s.shape[0] // NSLAB

    @pl.when(i == 0)
    def _():
        zero_s[...] = jnp.zeros_like(zero_s)

        def fill(e, c):
            pltpu.make_async_copy(zero_s, _slab(xs_ref, padstart_ref[e], nfill), zsem).start()
            return c

        lax.fori_loop(0, ne, fill, 0)

        def drain(e, c):
            pltpu.make_async_copy(zero_s, _slab(xs_ref, 0, nfill), zsem).wait()
            return c

        lax.fori_loop(0, ne, drain, 0)

    for part in range(tt // ct):
        def send(t, c, part=part):
            for k in range(TOP_K):
                pltpu.make_async_copy(_slab(hf_ref, part * ct + t, 1),
                                      _slab(xs_ref, dest_ref[part, k, t], 1), sem).start()
            return c

        lax.fori_loop(0, ct, send, 0)
    for k in range(TOP_K):
        pltpu.make_async_copy(hf_ref, _slab(xs_ref, 0, tt), sem).wait()


def _dispatch(dest3, padstart, hfs, p_rows):
    t = hfs.shape[0] // NSLAB
    tt = ROUTE_TILE
    parts = tt // dest3.shape[2]
    return pl.pallas_call(
        _dispatch_body,
        grid=(t // tt,),
        in_specs=[pl.BlockSpec((parts, TOP_K, dest3.shape[2]), lambda i: (i, 0, 0), memory_space=pltpu.SMEM),
                  pl.BlockSpec(memory_space=pltpu.SMEM),
                  pl.BlockSpec((tt * NSLAB, LANE), lambda i: (i, 0))],
        out_specs=pl.BlockSpec(memory_space=pl.ANY),
        out_shape=jax.ShapeDtypeStruct(((p_rows + MOE_BLOCK) * NSLAB, LANE), hfs.dtype),
        scratch_shapes=[pltpu.VMEM((MOE_BLOCK * NSLAB, LANE), hfs.dtype), pltpu.SemaphoreType.DMA,
                        pltpu.SemaphoreType.DMA],
        compiler_params=_cparams("arbitrary"),
        name="dispatch",
    )(dest3, padstart, hfs)


def _experts_body(bstart_ref, nblk_ref, nused_ref, xs_ref, wg_ref, wu_ref, wd_ref, eo_ref,
                  xbuf, obuf, stage_s, wgu_s, wd_s, isem, osem):
    e = pl.program_id(0)
    blk = stage_s.shape[0] // NSLAB
    f = wg_ref.shape[2]
    nused = nused_ref[0]

    def in_copy(g):
        slot = lax.rem(g, IN_SLOTS)
        return pltpu.make_async_copy(_slab(xs_ref, g * blk, blk), _slab(xbuf, slot * blk, blk), isem.at[slot])

    def out_copy(g):
        slot = g & 1
        return pltpu.make_async_copy(_slab(obuf, slot * blk, blk), _slab(eo_ref, g * blk, blk), osem.at[slot])

    @pl.when(e == 0)
    def _():
        for g in range(IN_SLOTS - 1):
            pl.when(g < nused)(lambda g=g: in_copy(jnp.int32(g)).start())

    @pl.when(nblk_ref[e] > 0)
    def _():
        wgu_s[:, 0:f] = wg_ref[0].astype(bf16)
        wgu_s[:, f:2 * f] = wu_ref[0].astype(bf16)
        wd_s[...] = wd_ref[0].astype(bf16)

        def block(g, c):
            in_copy(g).wait()

            @pl.when(g + IN_SLOTS - 1 < nused)
            def _():
                in_copy(g + IN_SLOTS - 1).start()

            islot = pl.multiple_of(lax.rem(g, IN_SLOTS) * (blk * NSLAB), blk * NSLAB)
            stage_s[...] = xbuf[pl.ds(islot, blk * NSLAB), :].astype(f32)
            x = _load_rows(stage_s, 0, blk).astype(bf16)
            gu = jnp.dot(x, wgu_s[...], preferred_element_type=f32)
            gate = gu[:, 0:f]
            h = (gate * jax.nn.sigmoid(gate) * gu[:, f:2 * f]).astype(bf16)
            o = jnp.dot(h, wd_s[...], preferred_element_type=f32)

            @pl.when(g >= 2)
            def _():
                out_copy(g - 2).wait()

            _store_rows(obuf, (g & 1) * blk, o)
            out_copy(g).start()
            return c

        lax.fori_loop(bstart_ref[e], bstart_ref[e] + nblk_ref[e], block, 0)

    @pl.when(e == pl.num_programs(0) - 1)
    def _():
        out_copy(nused - 1).wait()

        @pl.when(nused >= 2)
        def _():
            out_copy(nused - 2).wait()


def _experts(bstart, nblk, nused, xs, w_gate, w_up, w_down, nb):
    ne, d, f = w_gate.shape
    assert d == NSLAB * LANE
    blk = MOE_BLOCK
    w_map = lambda e, *_: (e, 0, 0)
    grid_spec = pltpu.PrefetchScalarGridSpec(
        num_scalar_prefetch=3,
        grid=(ne,),
        in_specs=[pl.BlockSpec(memory_space=pl.ANY),
                  pl.BlockSpec((1, d, f), w_map),
                  pl.BlockSpec((1, d, f), w_map),
                  pl.BlockSpec((1, f, d), w_map)],
        out_specs=pl.BlockSpec(memory_space=pl.ANY),
        scratch_shapes=[pltpu.VMEM((IN_SLOTS * blk * NSLAB, LANE), xs.dtype),
                        pltpu.VMEM((2 * blk * NSLAB, LANE), f32),
                        pltpu.VMEM((blk * NSLAB, LANE), f32),
                        pltpu.VMEM((d, 2 * f), bf16), pltpu.VMEM((f, d), bf16),
                        pltpu.SemaphoreType.DMA((IN_SLOTS,)), pltpu.SemaphoreType.DMA((2,))],
    )
    return pl.pallas_call(
        _experts_body,
        grid_spec=grid_spec,
        out_shape=jax.ShapeDtypeStruct((nb * blk * NSLAB, LANE), f32),
        compiler_params=_cparams("arbitrary"),
        name="experts",
    )(bstart, nblk, nused, xs, w_gate, w_up, w_down)


def _combine_body(dest_ref, destn_ref, eo_ref, x1_ref, hp_ref, w_ref, mod_ref, wsg_ref, wsu_ref, wsd_ref,
                  nf_ref, y_ref, buf, stage_s, sem, *, chunk_seq):
    i = pl.program_id(0)
    tt, d = x1_ref.shape
    slot = i & 1

    def gather(idx_ref, into):
        def fetch(t, c):
            for k in range(TOP_K):
                pltpu.make_async_copy(_slab(eo_ref, idx_ref[0, k, t], 1),
                                      _slab(buf, (into * TOP_K + k) * tt + t, 1), sem.at[into]).start()
            return c

        lax.fori_loop(0, tt, fetch, 0)

    pl.when(i == 0)(lambda: gather(dest_ref, slot))
    pl.when(i + 1 < pl.num_programs(0))(lambda: gather(destn_ref, 1 - slot))

    stage_s[...] = hp_ref[...].astype(f32)
    hf = _load_rows(stage_s, 0, tt).astype(bf16)
    g = jnp.dot(hf, wsg_ref[...], preferred_element_type=f32)
    u = jnp.dot(hf, wsu_ref[...], preferred_element_type=f32)
    acc = jnp.dot((g * jax.nn.sigmoid(g) * u).astype(bf16), wsd_ref[...], preferred_element_type=f32)

    for k in range(TOP_K):
        pltpu.make_async_copy(_slab(eo_ref, 0, tt), _slab(buf, (slot * TOP_K + k) * tt, tt), sem.at[slot]).wait()
    for k in range(TOP_K):
        acc = acc + _load_rows(buf, (slot * TOP_K + k) * tt, tt) * w_ref[:, k:k + 1]

    for j in range(tt // CHUNK):
        rows = pl.ds(j * CHUNK, CHUNK)
        g2 = mod_ref[j if chunk_seq else 0][5:6, :]
        y = x1_ref[rows, :] + g2 * acc[j * CHUNK:(j + 1) * CHUNK, :]
        y_ref[rows, :] = _rms(y) * nf_ref[...]


def _combine(dest3, eo, x1, hp, wsel, mod, p, nseq, seq_len, tile_offset):
    d = x1.shape[1]
    tt = COMB_TILE
    f = p["ws_gate"].shape[1]
    t = nseq * seq_len
    chunk_seq = seq_len == CHUNK
    if chunk_seq:
        spt, tiles_per_seq = tt // CHUNK, 1
        seq_of = lambda i: i
    else:
        assert seq_len % tt == 0
        spt, tiles_per_seq = 1, seq_len // tt
        seq_of = lambda i: i // tiles_per_seq
    body = functools.partial(_combine_body, chunk_seq=chunk_seq)
    const = lambda shape: pl.BlockSpec(shape, lambda i: (0,) * len(shape))
    tok = lambda w: pl.BlockSpec((tt, w), lambda i: (i + tile_offset, 0))
    ntiles = t // tt
    return pl.pallas_call(
        body,
        grid=(ntiles,),
        in_specs=[pl.BlockSpec((1, TOP_K, tt), lambda i: (i + tile_offset, 0, 0), memory_space=pltpu.SMEM),
                  pl.BlockSpec((1, TOP_K, tt), lambda i: (jnp.minimum(i + 1, ntiles - 1) + tile_offset, 0, 0),
                               memory_space=pltpu.SMEM),
                  pl.BlockSpec(memory_space=pl.ANY),
                  tok(d), pl.BlockSpec((tt * NSLAB, LANE), lambda i: (i + tile_offset, 0)), tok(TOP_K),
                  pl.BlockSpec((spt, 6, d), lambda i: (seq_of(i), 0, 0)),
                  const((d, f)), const((d, f)), const((f, d)), const((1, d))],
        out_specs=pl.BlockSpec((tt, d), lambda i: (i, 0)),
        out_shape=jax.ShapeDtypeStruct((t, d), f32),
        scratch_shapes=[pltpu.VMEM((2 * TOP_K * tt * NSLAB, LANE), eo.dtype),
                        pltpu.VMEM((tt * NSLAB, LANE), f32), pltpu.SemaphoreType.DMA((2,))],
        compiler_params=_cparams("arbitrary"),
        name="combine_sample" if chunk_seq else "combine_prompt",
    )(dest3, dest3, eo, x1, hp, wsel, mod, p["ws_gate"], p["ws_up"], p["ws_down"], p["norm_final"])


def kernel(x_prompt, x_sample, state_hgrn, state_gla, c_prompt, c_sample, w_ada, b_ada, norm1, norm2, w_in,
           hg_lb_logits, gla_gk_up, gla_gk_bias, hg_out_norm, gla_out_norm, w_out, w_router, b_router,
           w_gate, w_up, w_down, ws_gate, ws_up, ws_down, norm_final):
    nbp, lp, d = x_prompt.shape
    nbs, ls, _ = x_sample.shape
    depth = w_in.shape[0]
    assert depth == 1
    ne = w_router.shape[2]
    hgw = hg_out_norm.shape[1]
    gkw = gla_gk_up.shape[2]
    rank = gla_gk_up.shape[1]
    d_in = w_in.shape[2]

    nz = d_in - rank + LANE
    w_in_p = jnp.pad(w_in[0].astype(bf16), ((0, 0), (0, nz - d_in)))
    p = dict(
        norm1=norm1[0].reshape(1, d), norm2=norm2[0].reshape(1, d), w_in=w_in_p,
        lb_logits=hg_lb_logits,
        gk_up=jnp.zeros((LANE, gkw), f32).at[:rank].set(gla_gk_up[0]),
        gk_bias=gla_gk_bias[0].reshape(1, gkw),
        hg_norm=hg_out_norm[0].reshape(1, hgw), gla_norm=gla_out_norm[0].reshape(1, -1),
        w_out=w_out[0].astype(bf16), w_rt=w_router[0].T,
        ws_gate=ws_gate[0].astype(bf16), ws_up=ws_up[0].astype(bf16), ws_down=ws_down[0].astype(bf16),
        norm_final=norm_final.reshape(1, d),
    )

    c_all = jnp.concatenate([c_prompt, c_sample], axis=0)
    mod = _ada(c_all, w_ada[0], b_ada[0]).reshape(nbp + nbs, 6, d)

    zero_hg = jnp.zeros((nbp,) + state_hgrn.shape[2:], f32)
    zero_gl = jnp.zeros((nbp,) + state_gla.shape[2:], f32)
    tp, ts = nbp * lp, nbs * ls
    t = tp + ts
    x1, hp, lgt, hg_p, gl_p = _mixer(x_prompt, mod[:nbp], zero_hg, zero_gl, p, False, t, 0)
    x1, hp, lgt, hg_s, gl_s = _mixer(x_sample, mod[nbp:], state_hgrn[0], state_gla[0], p, True, t,
                                     tp // MIX_TILE, prev=(x1, hp, lgt))

    eidx_t, wsel_t, rank_t, cnt = _route(lgt, b_router[0])

    counts = cnt[:, 0]
    padded = (counts + MOE_BLOCK - 1) // MOE_BLOCK * MOE_BLOCK
    pends = jnp.cumsum(padded)
    pstarts = pends - padded
    nb = -(-(t * TOP_K + ne * (MOE_BLOCK - 1)) // MOE_BLOCK)
    nused = (pends[-1] // MOE_BLOCK).astype(i32).reshape(1)

    dest3 = _dest(eidx_t, rank_t, pstarts)
    xs = _dispatch(dest3, (pstarts + counts).astype(i32), hp, nb * MOE_BLOCK)
    eo = _experts((pstarts // MOE_BLOCK).astype(i32), (padded // MOE_BLOCK).astype(i32), nused, xs,
                  w_gate[0], w_up[0], w_down[0], nb)
    wsel = wsel_t.T
    y_prompt = _combine(dest3, eo, x1, hp, wsel, mod[:nbp], p, nbp, lp, 0)
    y_sample = _combine(dest3, eo, x1, hp, wsel, mod[nbp:], p, nbs, ls, tp // COMB_TILE)
    return (y_prompt.reshape(nbp, lp, d), y_sample.reshape(nbs, ls, d),
            hg_p[None], gl_p[None], hg_s[None], gl_s[None])
```

```python
import functools

import jax
import jax.numpy as jnp
from jax import lax
from jax.experimental import pallas as pl
from jax.experimental.pallas import tpu as pltpu

f32 = jnp.float32
bf16 = jnp.bfloat16
i32 = jnp.int32
HIGHEST = lax.Precision.HIGHEST

EPS = 1e-6
CHUNK = 64
HG_HEADS = 4
GLA_HEADS = 4
GLA_GATE_NORMALIZER = 16.0
TOP_K = 8
N_GROUPS = 8
TOPK_GROUPS = 4
ROUTE_SCALE = 2.5

LANE = 128
SUBLANE = 8
V7X_VMEM_BYTES = 64 * 1024 * 1024
VMEM_LIMIT = V7X_VMEM_BYTES - 8 * 1024 * 1024

MIX_TILE = 512
ROUTE_TILE = 512
COMB_TILE = 256
MOE_BLOCK = 256
IN_SLOTS = 3


def _cparams(*sem):
    return pltpu.CompilerParams(dimension_semantics=sem, vmem_limit_bytes=VMEM_LIMIT)


def _sigmoid_pair(x):
    e = jnp.exp(-jnp.abs(x))
    r = 1.0 / (1.0 + e)
    er = e * r
    pos = x >= 0
    return jnp.where(pos, r, er), jnp.where(pos, er, r)


def _rms(x):
    return x * lax.rsqrt(jnp.mean(x * x, axis=-1, keepdims=True) + EPS)


def _split(x):
    hi = x.astype(bf16)
    return hi, (x - hi.astype(f32)).astype(bf16)


def _dot_split(a, b_hi, b_lo, dims):
    a_hi, a_lo = _split(a)
    dg = functools.partial(lax.dot_general, dimension_numbers=dims, preferred_element_type=f32)
    return dg(a_hi, b_hi) + (dg(a_lo, b_hi) + dg(a_hi, b_lo))


NSLAB = SUBLANE


def _slab(ref, row0, n):
    return ref.at[pl.ds(pl.multiple_of(row0 * NSLAB, NSLAB), n * NSLAB), :]


def _load_rows(ref, row0, n):
    return jnp.concatenate([ref[pl.ds(row0 * NSLAB + s, n, stride=NSLAB), :] for s in range(NSLAB)], axis=1)


def _store_rows(ref, row0, x):
    for s in range(NSLAB):
        ref[pl.ds(row0 * NSLAB + s, x.shape[0], stride=NSLAB), :] = x[:, s * LANE:(s + 1) * LANE]


def _ada_body(c_ref, w_ref, b_ref, o_ref):
    c = c_ref[...]
    a = c * jax.nn.sigmoid(c)
    o_ref[...] = jnp.dot(a, w_ref[...], preferred_element_type=f32, precision=HIGHEST) + b_ref[...]


def _ada(c_all, w, b):
    ns, d = c_all.shape
    n = w.shape[1]
    tn = 1024
    return pl.pallas_call(
        _ada_body,
        grid=(n // tn,),
        in_specs=[pl.BlockSpec((ns, d), lambda j: (0, 0)),
                  pl.BlockSpec((d, tn), lambda j: (0, j)),
                  pl.BlockSpec((1, tn), lambda j: (0, j))],
        out_specs=pl.BlockSpec((ns, tn), lambda j: (0, j)),
        out_shape=jax.ShapeDtypeStruct((ns, n), f32),
        compiler_params=_cparams("arbitrary"),
        name="ada",
    )(c_all, w, b.reshape(1, n))


LEVELS = tuple(1 << n for n in range(CHUNK.bit_length() - 1))
NT = (((1,), (1,)), ((), ()))
TN = (((0,), (0,)), ((), ()))


def _pair_masks(c):
    t = lax.broadcasted_iota(i32, (c, c), 0)
    s = lax.broadcasted_iota(i32, (c, c), 1)
    x = t ^ s
    masks = [(x == 0).astype(f32)]
    for m in LEVELS:
        masks.append(((t > s) & (x >= m) & (x < 2 * m)).astype(f32))
    return masks


def _chunk_scan(q, k, g):
    c = q.shape[0]
    row = lax.broadcasted_iota(i32, q.shape, 0)
    pre = g
    tot = g
    zs = []
    for m in LEVELS:
        upper = (row & (2 * m - 1)) >= m
        e = jnp.exp(jnp.where(upper, pre, tot - pre))
        zs.append((jnp.where(upper, q, k) * e).astype(bf16))
        below = pltpu.roll(tot, m, 0)
        above = pltpu.roll(tot, c - m, 0)
        pre = pre + jnp.where(upper, below, 0.0)
        tot = tot + jnp.where(upper, below, above)
    qb = (q * jnp.exp(pre)).astype(bf16)
    kd = (k * jnp.exp(tot - pre)).astype(bf16)
    return zs, qb, kd, tot[0:1, :]


def _chunk_head(zs, qd, kd0, qb, kd, decay, vt, st_ref, slot, head, keep, lvl):
    own = (lambda a: a) if keep is None else (lambda a: a * keep)
    att = lax.dot_general(own(qd), kd0, NT, preferred_element_type=f32) * lvl[0]
    for z, mask in zip(zs, lvl[1:]):
        att = att + lax.dot_general(own(z), z, NT, preferred_element_type=f32) * mask
    st = st_ref[slot, head]
    lhs = jnp.concatenate([own(qb), att.astype(bf16)], axis=1)
    rhs_t = jnp.concatenate([st.astype(bf16), vt], axis=1)
    o = lax.dot_general(lhs, rhs_t, NT, preferred_element_type=f32)
    st_ref[slot, head] = st * decay + jnp.dot(vt, own(kd), preferred_element_type=f32)
    return o


def _mixer_body(*refs, nch, chunk_seq, tiles_per_seq, nprev):
    (x_ref, mod_ref, hg0_ref, gl0_ref, n1_ref, n2_ref, win_ref, lbl_ref, gup_ref, gbias_ref,
     hgn_ref, gln_ref, wout_ref, wrt_ref) = refs[:14]
    (x1_ref, hp_ref, lgt_ref, hg_ref, gl_ref, h_s, z_s, o_s, hf_s, slab_s, hgt_s, glt_s) = refs[14 + nprev:]
    i = pl.program_id(0)
    d = x_ref.shape[1]
    hdk, hdv = hg0_ref.shape[2], hg0_ref.shape[3]
    gdk, gdv = gl0_ref.shape[2], gl0_ref.shape[3]
    assert hdk == LANE and LANE % gdk == 0
    hgw = HG_HEADS * hdk
    gkw = GLA_HEADS * gdk
    gvw = GLA_HEADS * gdv
    nslot = hgt_s.shape[0]
    gla_lanes = lambda h: pl.ds((h * gdk) % LANE, gdk)

    def load_state():
        for s in range(nslot):
            for h in range(HG_HEADS):
                hgt_s[s, h] = hg0_ref[s, h].T
            for h in range(GLA_HEADS):
                glt_s[s, h] = jnp.zeros(glt_s.shape[2:], f32)
                glt_s[s, h, :, gla_lanes(h)] = gl0_ref[s, h].T

    if chunk_seq:
        load_state()
    else:
        pl.when(i % tiles_per_seq == 0)(load_state)

    lbl = lbl_ref[...]
    lbe = jnp.exp(lbl - jnp.max(lbl, axis=0, keepdims=True))
    lb = lbe[0:1, :] / jnp.sum(lbe, axis=0, keepdims=True)

    for j in range(nch):
        rows = pl.ds(j * CHUNK, CHUNK)
        m = mod_ref[j if chunk_seq else 0]
        xn = _rms(x_ref[rows, :]) * n1_ref[...]
        h_s[rows, :] = (xn * (1.0 + m[1:2, :]) + m[0:1, :]).astype(bf16)

    nz = z_s.shape[1]
    step = 512
    for n0 in range(0, nz, step):
        n1 = min(n0 + step, nz)
        z_s[:, n0:n1] = jnp.dot(h_s[...], win_ref[:, n0:n1], preferred_element_type=f32)

    o_hq, o_hf, o_hi, o_hgate = 0, hgw, 2 * hgw, 3 * hgw
    o_gq = 4 * hgw
    o_gk = o_gq + gkw
    o_gv = o_gk + gkw
    o_ggate = o_gv + gvw
    o_glr = o_ggate + gvw

    lvl = _pair_masks(CHUNK)
    lane = lax.broadcasted_iota(i32, (CHUNK, LANE), 1)
    heads_per_block = LANE // gdk
    for j in range(nch):
        rows = pl.ds(j * CHUNK, CHUNK)
        slot = j if chunk_seq else 0
        for h in range(HG_HEADS):
            lbh = lb[:, h * hdk:(h + 1) * hdk]
            hq = z_s[rows, pl.ds(o_hq + h * hdk, hdk)]
            sig, nsig = _sigmoid_pair(z_s[rows, pl.ds(o_hf + h * hdk, hdk)])
            g = jnp.log(lbh + (1.0 - lbh) * sig)
            k = (1.0 - lbh) * nsig
            q = hq * jax.nn.sigmoid(hq) * (hdk ** -0.5)
            zs, qb, kd, b_last = _chunk_scan(q, k, g)
            v = z_s[rows, pl.ds(o_hi + h * hdv, hdv)].T.astype(bf16)
            o = _chunk_head(zs, q.astype(bf16), k.astype(bf16), qb, kd, jnp.exp(b_last), v,
                            hgt_s, slot, h, None, lvl)
            hgate = z_s[rows, pl.ds(o_hgate + h * hdv, hdv)]
            o = _rms(o) * hgn_ref[:, pl.ds(h * hdv, hdv)] * jax.nn.sigmoid(hgate)
            o_s[rows, pl.ds(h * hdv, hdv)] = o.astype(bf16)
        u = _dot_split(z_s[rows, pl.ds(o_glr, LANE)], gup_ref[0], gup_ref[1], (((1,), (0,)), ((), ())))
        u = u + gbias_ref[...]
        loga = (jnp.minimum(u, 0.0) - jnp.log1p(jnp.exp(-jnp.abs(u)))) * (1.0 / GLA_GATE_NORMALIZER)
        for blk in range(GLA_HEADS // heads_per_block):
            q = z_s[rows, pl.ds(o_gq + blk * LANE, LANE)] * (gdk ** -0.5)
            k = z_s[rows, pl.ds(o_gk + blk * LANE, LANE)]
            zs, qb, kd, b_last = _chunk_scan(q, k, loga[:, blk * LANE:(blk + 1) * LANE])
            qd, kd0, decay = q.astype(bf16), k.astype(bf16), jnp.exp(b_last)
            for h in range(blk * heads_per_block, (blk + 1) * heads_per_block):
                lo = (h * gdk) % LANE
                keep = ((lane >= lo) & (lane < lo + gdk)).astype(bf16)
                v = z_s[rows, pl.ds(o_gv + h * gdv, gdv)].T.astype(bf16)
                o = _chunk_head(zs, qd, kd0, qb, kd, decay, v, glt_s, slot, h, keep, lvl)
                ggate = z_s[rows, pl.ds(o_ggate + h * gdv, gdv)]
                o = _rms(o) * gln_ref[...] * (ggate * jax.nn.sigmoid(ggate))
                o_s[rows, pl.ds(hgw + h * gdv, gdv)] = o.astype(bf16)

    a = jnp.dot(o_s[...], wout_ref[...], preferred_element_type=f32)
    for j in range(nch):
        rows = pl.ds(j * CHUNK, CHUNK)
        m = mod_ref[j if chunk_seq else 0]
        x1 = x_ref[rows, :] + m[2:3, :] * a[j * CHUNK:(j + 1) * CHUNK, :]
        x1_ref[rows, :] = x1
        hf = _rms(x1) * n2_ref[...] * (1.0 + m[4:5, :]) + m[3:4, :]
        hf_s[rows, :] = hf
        _store_rows(slab_s, j * CHUNK, hf)
    hp_ref[...] = slab_s[...].astype(bf16)
    h_hi, h_lo = _split(hf_s[...])
    nt = functools.partial(lax.dot_general, dimension_numbers=NT, preferred_element_type=f32)
    lgt_ref[...] = nt(wrt_ref[0], h_hi) + (nt(wrt_ref[1], h_hi) + nt(wrt_ref[0], h_lo))

    def store_state():
        for s in range(nslot):
            for h in range(HG_HEADS):
                hg_ref[s, h] = hgt_s[s, h].T
            for h in range(GLA_HEADS):
                gl_ref[s, h] = glt_s[s, h, :, gla_lanes(h)].T

    if chunk_seq:
        store_state()
    else:
        pl.when(i % tiles_per_seq == tiles_per_seq - 1)(store_state)


def _mixer(x, mod, hg0, gl0, p, chunk_seq, t_total, tile_offset, prev=()):
    nseq, L, d = x.shape
    t = nseq * L
    tm = MIX_TILE
    nch = tm // CHUNK
    if chunk_seq:
        assert L == CHUNK and nseq % nch == 0
        spt, tiles_per_seq = nch, 1
        seq_of = lambda i: i
    else:
        assert L % tm == 0
        spt, tiles_per_seq = 1, L // tm
        seq_of = lambda i: i // tiles_per_seq
    ne = p["w_rt"].shape[1]
    nz = p["w_in"].shape[1]
    _, hh, hdk, hdv = hg0.shape
    _, gh, gdk, gdv = gl0.shape
    const = lambda shape: pl.BlockSpec(shape, lambda i: (0,) * len(shape))
    nprev = len(prev)
    n_in = 14
    body = functools.partial(_mixer_body, nch=nch, chunk_seq=chunk_seq, tiles_per_seq=tiles_per_seq,
                             nprev=nprev)
    return pl.pallas_call(
        body,
        grid=(t // tm,),
        in_specs=[pl.BlockSpec((tm, d), lambda i: (i, 0)),
                  pl.BlockSpec((spt, 6, d), lambda i: (seq_of(i), 0, 0)),
                  pl.BlockSpec((spt, hh, hdk, hdv), lambda i: (seq_of(i), 0, 0, 0)),
                  pl.BlockSpec((spt, gh, gdk, gdv), lambda i: (seq_of(i), 0, 0, 0)),
                  const((1, d)), const((1, d)), const((d, nz)), const(p["lb_logits"].shape),
                  const(p["gk_up"].shape), const(p["gk_bias"].shape), const(p["hg_norm"].shape),
                  const(p["gla_norm"].shape), const((d, d)), const((2, ne, d))]
                 + [pl.BlockSpec(memory_space=pl.ANY)] * nprev,
        out_specs=[pl.BlockSpec((tm, d), lambda i: (i + tile_offset, 0)),
                   pl.BlockSpec((tm * NSLAB, LANE), lambda i: (i + tile_offset, 0)),
                   pl.BlockSpec((ne, tm), lambda i: (0, i + tile_offset)),
                   pl.BlockSpec((spt, hh, hdk, hdv), lambda i: (seq_of(i), 0, 0, 0)),
                   pl.BlockSpec((spt, gh, gdk, gdv), lambda i: (seq_of(i), 0, 0, 0))],
        out_shape=[jax.ShapeDtypeStruct((t_total, d), f32),
                   jax.ShapeDtypeStruct((t_total * NSLAB, LANE), bf16),
                   jax.ShapeDtypeStruct((ne, t_total), f32),
                   jax.ShapeDtypeStruct(hg0.shape, f32),
                   jax.ShapeDtypeStruct(gl0.shape, f32)],
        input_output_aliases={n_in + j: j for j in range(nprev)},
        scratch_shapes=[pltpu.VMEM((tm, d), bf16),
                        pltpu.VMEM((tm, nz), f32),
                        pltpu.VMEM((tm, d), bf16),
                        pltpu.VMEM((tm, d), f32),
                        pltpu.VMEM((tm * NSLAB, LANE), f32),
                        pltpu.VMEM((spt, hh, hdv, hdk), f32),
                        pltpu.VMEM((spt, gh, gdv, LANE), f32)],
        compiler_params=_cparams("arbitrary"),
        name="mixer_sample" if chunk_seq else "mixer_prompt",
    )(x.reshape(t, d), mod, hg0, gl0, p["norm1"], p["norm2"], p["w_in"], p["lb_logits"], p["gk_up"],
      p["gk_bias"], p["hg_norm"], p["gla_norm"], p["w_out"], p["w_rt"], *prev)


def _route_body(lg_ref, bias_ref, eidx_ref, wsel_ref, rank_ref, cnt_ref, run_s, s_s, cand_s):
    i = pl.program_id(0)
    ne, tt = lg_ref.shape
    gsz = ne // N_GROUPS
    ninf = -jnp.inf

    @pl.when(i == 0)
    def _():
        run_s[...] = jnp.zeros_like(run_s)

    row = lax.broadcasted_iota(i32, (ne, tt), 0)
    rg = lax.broadcasted_iota(i32, (gsz, tt), 0)

    gs = []
    for g in range(N_GROUPS):
        grows = pl.ds(g * gsz, gsz)
        sg = jax.nn.sigmoid(lg_ref[grows, :])
        s_s[grows, :] = sg
        blk = sg + bias_ref[grows, :]
        m1 = jnp.max(blk, axis=0, keepdims=True)
        first = jnp.min(jnp.where(blk == m1, rg, gsz), axis=0, keepdims=True)
        m2 = jnp.max(jnp.where(rg == first, ninf, blk), axis=0, keepdims=True)
        gs.append(m1 + m2)

    picked = [jnp.zeros((1, tt), f32) for _ in range(N_GROUPS)]
    for _ in range(TOPK_GROUPS):
        cur = [jnp.where(picked[g] > 0.5, ninf, gs[g]) for g in range(N_GROUPS)]
        m = functools.reduce(jnp.maximum, cur)
        gi = jnp.full((1, tt), N_GROUPS, i32)
        for g in reversed(range(N_GROUPS)):
            gi = jnp.where(cur[g] == m, g, gi)
        picked = [jnp.where(gi == g, 1.0, picked[g]) for g in range(N_GROUPS)]

    for g in range(N_GROUPS):
        grows = pl.ds(g * gsz, gsz)
        allowed = jnp.broadcast_to(picked[g], (gsz, tt)) > 0.5
        cand_s[grows, :] = jnp.where(allowed, s_s[grows, :] + bias_ref[grows, :], ninf)
    cand = cand_s[...]
    s = s_s[...]

    eis, ws = [], []
    chosen = jnp.zeros((ne, tt), jnp.bool_)
    for _ in range(TOP_K):
        m = jnp.max(cand, axis=0, keepdims=True)
        ei = jnp.min(jnp.where(cand == m, row, ne), axis=0, keepdims=True)
        hit = row == ei
        ws.append(jnp.sum(jnp.where(hit, s, 0.0), axis=0, keepdims=True))
        cand = jnp.where(hit, ninf, cand)
        chosen = chosen | hit
        eis.append(ei)
    wsum = functools.reduce(jnp.add, ws)
    scale = ROUTE_SCALE / wsum

    onehot = chosen.astype(bf16)
    ti = lax.broadcasted_iota(i32, (tt, tt), 0)
    tj = lax.broadcasted_iota(i32, (tt, tt), 1)
    before = (ti < tj).astype(bf16)
    prior = jnp.dot(onehot, before, preferred_element_type=f32) + run_s[:, 0:1]
    ct = eidx_ref.shape[2]
    for k in range(TOP_K):
        hit = row == eis[k]
        rank = jnp.sum(jnp.where(hit, prior, 0.0), axis=0, keepdims=True).astype(i32)
        wsel_ref[k:k + 1, :] = ws[k] * scale
        for part in range(tt // ct):
            eidx_ref[part, k:k + 1, :] = eis[k][:, part * ct:(part + 1) * ct]
            rank_ref[part, k:k + 1, :] = rank[:, part * ct:(part + 1) * ct]
    run_s[...] = run_s[...] + jnp.sum(chosen.astype(f32), axis=1, keepdims=True)
    cnt_ref[...] = run_s[...].astype(i32)


def _route(lgt, b_router):
    ne, t = lgt.shape
    tt, ct = ROUTE_TILE, COMB_TILE
    tiled = pl.BlockSpec((tt // ct, TOP_K, ct), lambda i: (i, 0, 0))
    tiled_shape = jax.ShapeDtypeStruct((t // ct, TOP_K, ct), i32)
    return pl.pallas_call(
        _route_body,
        grid=(t // tt,),
        in_specs=[pl.BlockSpec((ne, tt), lambda i: (0, i)),
                  pl.BlockSpec((ne, 1), lambda i: (0, 0))],
        out_specs=[tiled,
                   pl.BlockSpec((TOP_K, tt), lambda i: (0, i)),
                   tiled,
                   pl.BlockSpec((ne, LANE), lambda i: (0, 0))],
        out_shape=[tiled_shape, jax.ShapeDtypeStruct((TOP_K, t), f32), tiled_shape,
                   jax.ShapeDtypeStruct((ne, LANE), i32)],
        scratch_shapes=[pltpu.VMEM((ne, LANE), f32), pltpu.VMEM((ne, tt), f32), pltpu.VMEM((ne, tt), f32)],
        compiler_params=_cparams("arbitrary"),
        name="route",
    )(lgt, b_router.reshape(ne, 1))


def _dispatch_body(eidx_ref, rank_ref, pstart_ref, padstart_ref, hf_ref, xs_ref, zero_s, sem, zsem):
    i = pl.program_id(0)
    tt = hf_ref.shape[0] // NSLAB
    ct = eidx_ref.shape[2]
    ne = padstart_ref.shape[0]
    nfill = zero_s.shape[0] // NSLAB

    @pl.when(i == 0)
    def _():
        zero_s[...] = jnp.zeros_like(zero_s)

        def fill(e, c):
            pltpu.make_async_copy(zero_s, _slab(xs_ref, padstart_ref[e], nfill), zsem).start()
            return c

        lax.fori_loop(0, ne, fill, 0)

        def drain(e, c):
            pltpu.make_async_copy(zero_s, _slab(xs_ref, 0, nfill), zsem).wait()
            return c

        lax.fori_loop(0, ne, drain, 0)

    for part in range(tt // ct):
        def send(t, c, part=part):
            for k in range(TOP_K):
                dest = pstart_ref[eidx_ref[part, k, t]] + rank_ref[part, k, t]
                pltpu.make_async_copy(_slab(hf_ref, part * ct + t, 1), _slab(xs_ref, dest, 1), sem).start()
            return c

        lax.fori_loop(0, ct, send, 0)
    for k in range(TOP_K):
        pltpu.make_async_copy(hf_ref, _slab(xs_ref, 0, tt), sem).wait()


def _dispatch(eidx3, rank3, pstart, padstart, hfs, p_rows):
    t = hfs.shape[0] // NSLAB
    tt = ROUTE_TILE
    ct = eidx3.shape[2]
    idx = pl.BlockSpec((tt // ct, TOP_K, ct), lambda i: (i, 0, 0), memory_space=pltpu.SMEM)
    return pl.pallas_call(
        _dispatch_body,
        grid=(t // tt,),
        in_specs=[idx, idx,
                  pl.BlockSpec(memory_space=pltpu.SMEM),
                  pl.BlockSpec(memory_space=pltpu.SMEM),
                  pl.BlockSpec((tt * NSLAB, LANE), lambda i: (i, 0))],
        out_specs=pl.BlockSpec(memory_space=pl.ANY),
        out_shape=jax.ShapeDtypeStruct(((p_rows + MOE_BLOCK) * NSLAB, LANE), hfs.dtype),
        scratch_shapes=[pltpu.VMEM((MOE_BLOCK * NSLAB, LANE), hfs.dtype), pltpu.SemaphoreType.DMA,
                        pltpu.SemaphoreType.DMA],
        compiler_params=_cparams("arbitrary"),
        name="dispatch",
    )(eidx3, rank3, pstart, padstart, hfs)


def _experts_body(bstart_ref, nblk_ref, nused_ref, xs_ref, wg_ref, wu_ref, wd_ref, eo_ref,
                  xbuf, obuf, stage_s, wgu_s, wd_s, isem, osem):
    e = pl.program_id(0)
    blk = stage_s.shape[0] // NSLAB
    f = wg_ref.shape[2]
    nused = nused_ref[0]

    def in_copy(g):
        slot = lax.rem(g, IN_SLOTS)
        return pltpu.make_async_copy(_slab(xs_ref, g * blk, blk), _slab(xbuf, slot * blk, blk), isem.at[slot])

    def out_copy(g):
        slot = g & 1
        return pltpu.make_async_copy(_slab(obuf, slot * blk, blk), _slab(eo_ref, g * blk, blk), osem.at[slot])

    @pl.when(e == 0)
    def _():
        for g in range(IN_SLOTS - 1):
            pl.when(g < nused)(lambda g=g: in_copy(jnp.int32(g)).start())

    @pl.when(nblk_ref[e] > 0)
    def _():
        wgu_s[:, 0:f] = wg_ref[0].astype(bf16)
        wgu_s[:, f:2 * f] = wu_ref[0].astype(bf16)
        wd_s[...] = wd_ref[0].astype(bf16)

        def block(g, c):
            in_copy(g).wait()

            @pl.when(g + IN_SLOTS - 1 < nused)
            def _():
                in_copy(g + IN_SLOTS - 1).start()

            islot = pl.multiple_of(lax.rem(g, IN_SLOTS) * (blk * NSLAB), blk * NSLAB)
            stage_s[...] = xbuf[pl.ds(islot, blk * NSLAB), :].astype(f32)
            x = _load_rows(stage_s, 0, blk).astype(bf16)
            gu = jnp.dot(x, wgu_s[...], preferred_element_type=f32)
            gate = gu[:, 0:f]
            h = (gate * jax.nn.sigmoid(gate) * gu[:, f:2 * f]).astype(bf16)
            o = jnp.dot(h, wd_s[...], preferred_element_type=f32)

            @pl.when(g >= 2)
            def _():
                out_copy(g - 2).wait()

            _store_rows(obuf, (g & 1) * blk, o)
            out_copy(g).start()
            return c

        lax.fori_loop(bstart_ref[e], bstart_ref[e] + nblk_ref[e], block, 0)

    @pl.when(e == pl.num_programs(0) - 1)
    def _():
        out_copy(nused - 1).wait()

        @pl.when(nused >= 2)
        def _():
            out_copy(nused - 2).wait()


def _experts(bstart, nblk, nused, xs, w_gate, w_up, w_down, nb):
    ne, d, f = w_gate.shape
    assert d == NSLAB * LANE
    blk = MOE_BLOCK
    w_map = lambda e, *_: (e, 0, 0)
    grid_spec = pltpu.PrefetchScalarGridSpec(
        num_scalar_prefetch=3,
        grid=(ne,),
        in_specs=[pl.BlockSpec(memory_space=pl.ANY),
                  pl.BlockSpec((1, d, f), w_map),
                  pl.BlockSpec((1, d, f), w_map),
                  pl.BlockSpec((1, f, d), w_map)],
        out_specs=pl.BlockSpec(memory_space=pl.ANY),
        scratch_shapes=[pltpu.VMEM((IN_SLOTS * blk * NSLAB, LANE), xs.dtype),
                        pltpu.VMEM((2 * blk * NSLAB, LANE), f32),
                        pltpu.VMEM((blk * NSLAB, LANE), f32),
                        pltpu.VMEM((d, 2 * f), bf16), pltpu.VMEM((f, d), bf16),
                        pltpu.SemaphoreType.DMA((IN_SLOTS,)), pltpu.SemaphoreType.DMA((2,))],
    )
    return pl.pallas_call(
        _experts_body,
        grid_spec=grid_spec,
        out_shape=jax.ShapeDtypeStruct((nb * blk * NSLAB, LANE), f32),
        compiler_params=_cparams("arbitrary"),
        name="experts",
    )(bstart, nblk, nused, xs, w_gate, w_up, w_down)


def _combine_body(eidx_ref, rank_ref, eidxn_ref, rankn_ref, pstart_ref, eo_ref, x1_ref, hp_ref, w_ref, mod_ref,
                  wsg_ref, wsu_ref, wsd_ref, nf_ref, y_ref, buf, stage_s, sem, *, chunk_seq):
    i = pl.program_id(0)
    tt, d = x1_ref.shape
    slot = i & 1

    def gather(e_ref, r_ref, into):
        def fetch(t, c):
            for k in range(TOP_K):
                dest = pstart_ref[e_ref[0, k, t]] + r_ref[0, k, t]
                pltpu.make_async_copy(_slab(eo_ref, dest, 1),
                                      _slab(buf, (into * TOP_K + k) * tt + t, 1), sem.at[into]).start()
            return c

        lax.fori_loop(0, tt, fetch, 0)

    pl.when(i == 0)(lambda: gather(eidx_ref, rank_ref, slot))
    pl.when(i + 1 < pl.num_programs(0))(lambda: gather(eidxn_ref, rankn_ref, 1 - slot))

    stage_s[...] = hp_ref[...].astype(f32)
    hf = _load_rows(stage_s, 0, tt).astype(bf16)
    g = jnp.dot(hf, wsg_ref[...], preferred_element_type=f32)
    u = jnp.dot(hf, wsu_ref[...], preferred_element_type=f32)
    acc = jnp.dot((g * jax.nn.sigmoid(g) * u).astype(bf16), wsd_ref[...], preferred_element_type=f32)

    for k in range(TOP_K):
        pltpu.make_async_copy(_slab(eo_ref, 0, tt), _slab(buf, (slot * TOP_K + k) * tt, tt), sem.at[slot]).wait()
    for k in range(TOP_K):
        acc = acc + _load_rows(buf, (slot * TOP_K + k) * tt, tt) * w_ref[:, k:k + 1]

    for j in range(tt // CHUNK):
        rows = pl.ds(j * CHUNK, CHUNK)
        g2 = mod_ref[j if chunk_seq else 0][5:6, :]
        y = x1_ref[rows, :] + g2 * acc[j * CHUNK:(j + 1) * CHUNK, :]
        y_ref[rows, :] = _rms(y) * nf_ref[...]


def _combine(eidx3, rank3, pstart, eo, x1, hp, wsel, mod, p, nseq, seq_len, tile_offset):
    d = x1.shape[1]
    tt = COMB_TILE
    f = p["ws_gate"].shape[1]
    t = nseq * seq_len
    chunk_seq = seq_len == CHUNK
    if chunk_seq:
        spt, tiles_per_seq = tt // CHUNK, 1
        seq_of = lambda i: i
    else:
        assert seq_len % tt == 0
        spt, tiles_per_seq = 1, seq_len // tt
        seq_of = lambda i: i // tiles_per_seq
    body = functools.partial(_combine_body, chunk_seq=chunk_seq)
    const = lambda shape: pl.BlockSpec(shape, lambda i: (0,) * len(shape))
    tok = lambda w: pl.BlockSpec((tt, w), lambda i: (i + tile_offset, 0))
    ntiles = t // tt
    cur = pl.BlockSpec((1, TOP_K, tt), lambda i: (i + tile_offset, 0, 0), memory_space=pltpu.SMEM)
    nxt = pl.BlockSpec((1, TOP_K, tt), lambda i: (jnp.minimum(i + 1, ntiles - 1) + tile_offset, 0, 0),
                       memory_space=pltpu.SMEM)
    return pl.pallas_call(
        body,
        grid=(ntiles,),
        in_specs=[cur, cur, nxt, nxt,
                  pl.BlockSpec(memory_space=pltpu.SMEM),
                  pl.BlockSpec(memory_space=pl.ANY),
                  tok(d), pl.BlockSpec((tt * NSLAB, LANE), lambda i: (i + tile_offset, 0)), tok(TOP_K),
                  pl.BlockSpec((spt, 6, d), lambda i: (seq_of(i), 0, 0)),
                  const((d, f)), const((d, f)), const((f, d)), const((1, d))],
        out_specs=pl.BlockSpec((tt, d), lambda i: (i, 0)),
        out_shape=jax.ShapeDtypeStruct((t, d), f32),
        scratch_shapes=[pltpu.VMEM((2 * TOP_K * tt * NSLAB, LANE), eo.dtype),
                        pltpu.VMEM((tt * NSLAB, LANE), f32), pltpu.SemaphoreType.DMA((2,))],
        compiler_params=_cparams("arbitrary"),
        name="combine_sample" if chunk_seq else "combine_prompt",
    )(eidx3, rank3, eidx3, rank3, pstart, eo, x1, hp, wsel, mod, p["ws_gate"], p["ws_up"], p["ws_down"],
      p["norm_final"])


def kernel(x_prompt, x_sample, state_hgrn, state_gla, c_prompt, c_sample, w_ada, b_ada, norm1, norm2, w_in,
           hg_lb_logits, gla_gk_up, gla_gk_bias, hg_out_norm, gla_out_norm, w_out, w_router, b_router,
           w_gate, w_up, w_down, ws_gate, ws_up, ws_down, norm_final):
    nbp, lp, d = x_prompt.shape
    nbs, ls, _ = x_sample.shape
    depth = w_in.shape[0]
    assert depth == 1
    ne = w_router.shape[2]
    hgw = hg_out_norm.shape[1]
    gkw = gla_gk_up.shape[2]
    rank = gla_gk_up.shape[1]
    d_in = w_in.shape[2]

    nz = d_in - rank + LANE
    w_in_p = jnp.pad(w_in[0].astype(bf16), ((0, 0), (0, nz - d_in)))
    p = dict(
        norm1=norm1[0].reshape(1, d), norm2=norm2[0].reshape(1, d), w_in=w_in_p,
        lb_logits=hg_lb_logits,
        gk_up=jnp.stack(_split(jnp.pad(gla_gk_up[0], ((0, LANE - rank), (0, 0))))),
        gk_bias=gla_gk_bias[0].reshape(1, gkw),
        hg_norm=hg_out_norm[0].reshape(1, hgw), gla_norm=gla_out_norm[0].reshape(1, -1),
        w_out=w_out[0].astype(bf16), w_rt=jnp.stack(_split(w_router[0].T)),
        ws_gate=ws_gate[0].astype(bf16), ws_up=ws_up[0].astype(bf16), ws_down=ws_down[0].astype(bf16),
        norm_final=norm_final.reshape(1, d),
    )

    c_all = jnp.concatenate([c_prompt, c_sample], axis=0)
    mod = _ada(c_all, w_ada[0], b_ada[0]).reshape(nbp + nbs, 6, d)

    zero_hg = jnp.zeros((nbp,) + state_hgrn.shape[2:], f32)
    zero_gl = jnp.zeros((nbp,) + state_gla.shape[2:], f32)
    tp, ts = nbp * lp, nbs * ls
    t = tp + ts
    x1, hp, lgt, hg_p, gl_p = _mixer(x_prompt, mod[:nbp], zero_hg, zero_gl, p, False, t, 0)
    x1, hp, lgt, hg_s, gl_s = _mixer(x_sample, mod[nbp:], state_hgrn[0], state_gla[0], p, True, t,
                                     tp // MIX_TILE, prev=(x1, hp, lgt))

    eidx3, wsel_t, rank3, cnt = _route(lgt, b_router[0])

    counts = cnt[:, 0]
    padded = (counts + MOE_BLOCK - 1) // MOE_BLOCK * MOE_BLOCK
    pends = jnp.cumsum(padded)
    pstarts = pends - padded
    nb = -(-(t * TOP_K + ne * (MOE_BLOCK - 1)) // MOE_BLOCK)
    nused = (pends[-1] // MOE_BLOCK).astype(i32).reshape(1)

    pstarts = pstarts.astype(i32)
    xs = _dispatch(eidx3, rank3, pstarts, (pstarts + counts).astype(i32), hp, nb * MOE_BLOCK)
    eo = _experts(pstarts // MOE_BLOCK, (padded // MOE_BLOCK).astype(i32), nused, xs,
                  w_gate[0], w_up[0], w_down[0], nb)
    wsel = wsel_t.T
    y_prompt = _combine(eidx3, rank3, pstarts, eo, x1, hp, wsel, mod[:nbp], p, nbp, lp, 0)
    y_sample = _combine(eidx3, rank3, pstarts, eo, x1, hp, wsel, mod[nbp:], p, nbs, ls, tp // COMB_TILE)
    return (y_prompt.reshape(nbp, lp, d), y_sample.reshape(nbs, ls, d),
            hg_p[None], gl_p[None], hg_s[None], gl_s[None])
```

```python
import functools

import jax
import jax.numpy as jnp
from jax import lax
from jax.experimental import pallas as pl
from jax.experimental.pallas import tpu as pltpu

f32 = jnp.float32
bf16 = jnp.bfloat16
i32 = jnp.int32
HIGHEST = lax.Precision.HIGHEST

EPS = 1e-6
CHUNK = 64
HG_HEADS = 4
GLA_HEADS = 4
GLA_GATE_NORMALIZER = 16.0
TOP_K = 8
N_GROUPS = 8
TOPK_GROUPS = 4
ROUTE_SCALE = 2.5

LANE = 128
SUBLANE = 8
V7X_VMEM_BYTES = 64 * 1024 * 1024
VMEM_LIMIT = V7X_VMEM_BYTES - 8 * 1024 * 1024

MIX_TILE = 512
ROUTE_TILE = 512
COMB_TILE = 256
MOE_BLOCK = 256
IN_SLOTS = 3
OUT_SLOTS = 3


def _cparams(*sem):
    return pltpu.CompilerParams(dimension_semantics=sem, vmem_limit_bytes=VMEM_LIMIT)


def _sigmoid_pair(x):
    e = jnp.exp(-jnp.abs(x))
    r = 1.0 / (1.0 + e)
    er = e * r
    pos = x >= 0
    return jnp.where(pos, r, er), jnp.where(pos, er, r)


def _rms(x):
    return x * lax.rsqrt(jnp.mean(x * x, axis=-1, keepdims=True) + EPS)


def _split(x):
    hi = x.astype(bf16)
    return hi, (x - hi.astype(f32)).astype(bf16)


def _dot_split(a, b_hi, b_lo, dims):
    a_hi, a_lo = _split(a)
    dg = functools.partial(lax.dot_general, dimension_numbers=dims, preferred_element_type=f32)
    return dg(a_hi, b_hi) + (dg(a_lo, b_hi) + dg(a_hi, b_lo))


NSLAB = SUBLANE


def _slab(ref, row0, n):
    return ref.at[pl.ds(pl.multiple_of(row0 * NSLAB, NSLAB), n * NSLAB), :]


def _load_rows(ref, row0, n):
    return jnp.concatenate([ref[pl.ds(row0 * NSLAB + s, n, stride=NSLAB), :] for s in range(NSLAB)], axis=1)


def _store_rows(ref, row0, x):
    for s in range(NSLAB):
        ref[pl.ds(row0 * NSLAB + s, x.shape[0], stride=NSLAB), :] = x[:, s * LANE:(s + 1) * LANE]


def _ada_body(c_ref, w_ref, b_ref, o_ref):
    c = c_ref[...]
    a = c * jax.nn.sigmoid(c)
    o_ref[...] = jnp.dot(a, w_ref[...], preferred_element_type=f32, precision=HIGHEST) + b_ref[...]


def _ada(c_all, w, b):
    ns, d = c_all.shape
    n = w.shape[1]
    tn = 1024
    return pl.pallas_call(
        _ada_body,
        grid=(n // tn,),
        in_specs=[pl.BlockSpec((ns, d), lambda j: (0, 0)),
                  pl.BlockSpec((d, tn), lambda j: (0, j)),
                  pl.BlockSpec((1, tn), lambda j: (0, j))],
        out_specs=pl.BlockSpec((ns, tn), lambda j: (0, j)),
        out_shape=jax.ShapeDtypeStruct((ns, n), f32),
        compiler_params=_cparams("arbitrary"),
        name="ada",
    )(c_all, w, b.reshape(1, n))


LEVELS = tuple(1 << n for n in range(CHUNK.bit_length() - 1))
NT = (((1,), (1,)), ((), ()))
TN = (((0,), (0,)), ((), ()))


def _pair_masks(c):
    t = lax.broadcasted_iota(i32, (c, c), 0)
    s = lax.broadcasted_iota(i32, (c, c), 1)
    x = t ^ s
    masks = [(x == 0).astype(f32)]
    for m in LEVELS:
        masks.append(((t > s) & (x >= m) & (x < 2 * m)).astype(f32))
    return masks


def _chunk_scan(q, k, g):
    c = q.shape[0]
    row = lax.broadcasted_iota(i32, q.shape, 0)
    pre = g
    tot = g
    zs = []
    for m in LEVELS:
        upper = (row & (2 * m - 1)) >= m
        e = jnp.exp(jnp.where(upper, pre, tot - pre))
        zs.append((jnp.where(upper, q, k) * e).astype(bf16))
        below = pltpu.roll(tot, m, 0)
        above = pltpu.roll(tot, c - m, 0)
        pre = pre + jnp.where(upper, below, 0.0)
        tot = tot + jnp.where(upper, below, above)
    qb = (q * jnp.exp(pre)).astype(bf16)
    kd = (k * jnp.exp(tot - pre)).astype(bf16)
    return zs, qb, kd, tot[0:1, :]


def _chunk_head(zs, qd, kd0, qb, kd, decay, vt, st_ref, slot, head, keep, lvl):
    own = (lambda a: a) if keep is None else (lambda a: a * keep)
    att = lax.dot_general(own(qd), kd0, NT, preferred_element_type=f32) * lvl[0]
    for z, mask in zip(zs, lvl[1:]):
        att = att + lax.dot_general(own(z), z, NT, preferred_element_type=f32) * mask
    st = st_ref[slot, head]
    lhs = jnp.concatenate([own(qb), att.astype(bf16)], axis=1)
    rhs_t = jnp.concatenate([st.astype(bf16), vt], axis=1)
    o = lax.dot_general(lhs, rhs_t, NT, preferred_element_type=f32)
    st_ref[slot, head] = st * decay + jnp.dot(vt, own(kd), preferred_element_type=f32)
    return o


def _mixer_body(*refs, nch, chunk_seq, tiles_per_seq, nprev):
    (x_ref, mod_ref, hg0_ref, gl0_ref, n1_ref, n2_ref, win_ref, lbl_ref, gup_ref, gbias_ref,
     hgn_ref, gln_ref, wout_ref, wrt_ref) = refs[:14]
    (x1_ref, hp_ref, lgt_ref, hg_ref, gl_ref, h_s, z_s, o_s, hf_s, slab_s, hgt_s, glt_s) = refs[14 + nprev:]
    i = pl.program_id(0)
    d = x_ref.shape[1]
    hdk, hdv = hg0_ref.shape[2], hg0_ref.shape[3]
    gdk, gdv = gl0_ref.shape[2], gl0_ref.shape[3]
    assert hdk == LANE and LANE % gdk == 0
    hgw = HG_HEADS * hdk
    gkw = GLA_HEADS * gdk
    gvw = GLA_HEADS * gdv
    nslot = hgt_s.shape[0]
    gla_lanes = lambda h: pl.ds((h * gdk) % LANE, gdk)

    def load_state():
        for s in range(nslot):
            for h in range(HG_HEADS):
                hgt_s[s, h] = hg0_ref[s, h].T
            for h in range(GLA_HEADS):
                glt_s[s, h] = jnp.zeros(glt_s.shape[2:], f32)
                glt_s[s, h, :, gla_lanes(h)] = gl0_ref[s, h].T

    if chunk_seq:
        load_state()
    else:
        pl.when(i % tiles_per_seq == 0)(load_state)

    lbl = lbl_ref[...]
    lbe = jnp.exp(lbl - jnp.max(lbl, axis=0, keepdims=True))
    lb = lbe[0:1, :] / jnp.sum(lbe, axis=0, keepdims=True)

    for j in range(nch):
        rows = pl.ds(j * CHUNK, CHUNK)
        m = mod_ref[j if chunk_seq else 0]
        xn = _rms(x_ref[rows, :]) * n1_ref[...]
        h_s[rows, :] = (xn * (1.0 + m[1:2, :]) + m[0:1, :]).astype(bf16)

    nz = z_s.shape[1]
    step = 512
    for n0 in range(0, nz, step):
        n1 = min(n0 + step, nz)
        z_s[:, n0:n1] = jnp.dot(h_s[...], win_ref[:, n0:n1], preferred_element_type=f32)

    o_hq, o_hf, o_hi, o_hgate = 0, hgw, 2 * hgw, 3 * hgw
    o_gq = 4 * hgw
    o_gk = o_gq + gkw
    o_gv = o_gk + gkw
    o_ggate = o_gv + gvw
    o_glr = o_ggate + gvw

    lvl = _pair_masks(CHUNK)
    lane = lax.broadcasted_iota(i32, (CHUNK, LANE), 1)
    heads_per_block = LANE // gdk
    for j in range(nch):
        rows = pl.ds(j * CHUNK, CHUNK)
        slot = j if chunk_seq else 0
        for h in range(HG_HEADS):
            lbh = lb[:, h * hdk:(h + 1) * hdk]
            hq = z_s[rows, pl.ds(o_hq + h * hdk, hdk)]
            sig, nsig = _sigmoid_pair(z_s[rows, pl.ds(o_hf + h * hdk, hdk)])
            g = jnp.log(lbh + (1.0 - lbh) * sig)
            k = (1.0 - lbh) * nsig
            q = hq * jax.nn.sigmoid(hq) * (hdk ** -0.5)
            zs, qb, kd, b_last = _chunk_scan(q, k, g)
            v = z_s[rows, pl.ds(o_hi + h * hdv, hdv)].T.astype(bf16)
            o = _chunk_head(zs, q.astype(bf16), k.astype(bf16), qb, kd, jnp.exp(b_last), v,
                            hgt_s, slot, h, None, lvl)
            hgate = z_s[rows, pl.ds(o_hgate + h * hdv, hdv)]
            o = _rms(o) * hgn_ref[:, pl.ds(h * hdv, hdv)] * jax.nn.sigmoid(hgate)
            o_s[rows, pl.ds(h * hdv, hdv)] = o.astype(bf16)
        u = _dot_split(z_s[rows, pl.ds(o_glr, LANE)], gup_ref[0], gup_ref[1], (((1,), (0,)), ((), ())))
        u = u + gbias_ref[...]
        loga = (jnp.minimum(u, 0.0) - jnp.log1p(jnp.exp(-jnp.abs(u)))) * (1.0 / GLA_GATE_NORMALIZER)
        for blk in range(GLA_HEADS // heads_per_block):
            q = z_s[rows, pl.ds(o_gq + blk * LANE, LANE)] * (gdk ** -0.5)
            k = z_s[rows, pl.ds(o_gk + blk * LANE, LANE)]
            zs, qb, kd, b_last = _chunk_scan(q, k, loga[:, blk * LANE:(blk + 1) * LANE])
            qd, kd0, decay = q.astype(bf16), k.astype(bf16), jnp.exp(b_last)
            for h in range(blk * heads_per_block, (blk + 1) * heads_per_block):
                lo = (h * gdk) % LANE
                keep = ((lane >= lo) & (lane < lo + gdk)).astype(bf16)
                v = z_s[rows, pl.ds(o_gv + h * gdv, gdv)].T.astype(bf16)
                o = _chunk_head(zs, qd, kd0, qb, kd, decay, v, glt_s, slot, h, keep, lvl)
                ggate = z_s[rows, pl.ds(o_ggate + h * gdv, gdv)]
                o = _rms(o) * gln_ref[...] * (ggate * jax.nn.sigmoid(ggate))
                o_s[rows, pl.ds(hgw + h * gdv, gdv)] = o.astype(bf16)

    a = jnp.dot(o_s[...], wout_ref[...], preferred_element_type=f32)
    for j in range(nch):
        rows = pl.ds(j * CHUNK, CHUNK)
        m = mod_ref[j if chunk_seq else 0]
        x1 = x_ref[rows, :] + m[2:3, :] * a[j * CHUNK:(j + 1) * CHUNK, :]
        x1_ref[rows, :] = x1
        hf = _rms(x1) * n2_ref[...] * (1.0 + m[4:5, :]) + m[3:4, :]
        hf_s[rows, :] = hf
        _store_rows(slab_s, j * CHUNK, hf)
    hp_ref[...] = slab_s[...].astype(bf16)
    h_hi, h_lo = _split(hf_s[...])
    nt = functools.partial(lax.dot_general, dimension_numbers=NT, preferred_element_type=f32)
    lgt_ref[...] = nt(wrt_ref[0], h_hi) + (nt(wrt_ref[1], h_hi) + nt(wrt_ref[0], h_lo))

    def store_state():
        for s in range(nslot):
            for h in range(HG_HEADS):
                hg_ref[s, h] = hgt_s[s, h].T
            for h in range(GLA_HEADS):
                gl_ref[s, h] = glt_s[s, h, :, gla_lanes(h)].T

    if chunk_seq:
        store_state()
    else:
        pl.when(i % tiles_per_seq == tiles_per_seq - 1)(store_state)


def _mixer(x, mod, hg0, gl0, p, chunk_seq, t_total, tile_offset, prev=()):
    nseq, L, d = x.shape
    t = nseq * L
    tm = MIX_TILE
    nch = tm // CHUNK
    if chunk_seq:
        assert L == CHUNK and nseq % nch == 0
        spt, tiles_per_seq = nch, 1
        seq_of = lambda i: i
    else:
        assert L % tm == 0
        spt, tiles_per_seq = 1, L // tm
        seq_of = lambda i: i // tiles_per_seq
    ne = p["w_rt"].shape[1]
    nz = p["w_in"].shape[1]
    _, hh, hdk, hdv = hg0.shape
    _, gh, gdk, gdv = gl0.shape
    const = lambda shape: pl.BlockSpec(shape, lambda i: (0,) * len(shape))
    nprev = len(prev)
    n_in = 14
    body = functools.partial(_mixer_body, nch=nch, chunk_seq=chunk_seq, tiles_per_seq=tiles_per_seq,
                             nprev=nprev)
    return pl.pallas_call(
        body,
        grid=(t // tm,),
        in_specs=[pl.BlockSpec((tm, d), lambda i: (i, 0)),
                  pl.BlockSpec((spt, 6, d), lambda i: (seq_of(i), 0, 0)),
                  pl.BlockSpec((spt, hh, hdk, hdv), lambda i: (seq_of(i), 0, 0, 0)),
                  pl.BlockSpec((spt, gh, gdk, gdv), lambda i: (seq_of(i), 0, 0, 0)),
                  const((1, d)), const((1, d)), const((d, nz)), const(p["lb_logits"].shape),
                  const(p["gk_up"].shape), const(p["gk_bias"].shape), const(p["hg_norm"].shape),
                  const(p["gla_norm"].shape), const((d, d)), const((2, ne, d))]
                 + [pl.BlockSpec(memory_space=pl.ANY)] * nprev,
        out_specs=[pl.BlockSpec((tm, d), lambda i: (i + tile_offset, 0)),
                   pl.BlockSpec((tm * NSLAB, LANE), lambda i: (i + tile_offset, 0)),
                   pl.BlockSpec((ne, tm), lambda i: (0, i + tile_offset)),
                   pl.BlockSpec((spt, hh, hdk, hdv), lambda i: (seq_of(i), 0, 0, 0)),
                   pl.BlockSpec((spt, gh, gdk, gdv), lambda i: (seq_of(i), 0, 0, 0))],
        out_shape=[jax.ShapeDtypeStruct((t_total, d), f32),
                   jax.ShapeDtypeStruct((t_total * NSLAB, LANE), bf16),
                   jax.ShapeDtypeStruct((ne, t_total), f32),
                   jax.ShapeDtypeStruct(hg0.shape, f32),
                   jax.ShapeDtypeStruct(gl0.shape, f32)],
        input_output_aliases={n_in + j: j for j in range(nprev)},
        scratch_shapes=[pltpu.VMEM((tm, d), bf16),
                        pltpu.VMEM((tm, nz), f32),
                        pltpu.VMEM((tm, d), bf16),
                        pltpu.VMEM((tm, d), f32),
                        pltpu.VMEM((tm * NSLAB, LANE), f32),
                        pltpu.VMEM((spt, hh, hdv, hdk), f32),
                        pltpu.VMEM((spt, gh, gdv, LANE), f32)],
        compiler_params=_cparams("arbitrary"),
        name="mixer_sample" if chunk_seq else "mixer_prompt",
    )(x.reshape(t, d), mod, hg0, gl0, p["norm1"], p["norm2"], p["w_in"], p["lb_logits"], p["gk_up"],
      p["gk_bias"], p["hg_norm"], p["gla_norm"], p["w_out"], p["w_rt"], *prev)


def _route_body(lg_ref, bias_ref, eidx_ref, wsel_ref, rank_ref, cnt_ref, run_s, s_s, cand_s):
    i = pl.program_id(0)
    ne, tt = lg_ref.shape
    gsz = ne // N_GROUPS
    ninf = -jnp.inf

    @pl.when(i == 0)
    def _():
        run_s[...] = jnp.zeros_like(run_s)

    row = lax.broadcasted_iota(i32, (ne, tt), 0)
    rg = lax.broadcasted_iota(i32, (gsz, tt), 0)

    gs = []
    for g in range(N_GROUPS):
        grows = pl.ds(g * gsz, gsz)
        sg = jax.nn.sigmoid(lg_ref[grows, :])
        s_s[grows, :] = sg
        blk = sg + bias_ref[grows, :]
        m1 = jnp.max(blk, axis=0, keepdims=True)
        first = jnp.min(jnp.where(blk == m1, rg, gsz), axis=0, keepdims=True)
        m2 = jnp.max(jnp.where(rg == first, ninf, blk), axis=0, keepdims=True)
        gs.append(m1 + m2)

    picked = [jnp.zeros((1, tt), f32) for _ in range(N_GROUPS)]
    for _ in range(TOPK_GROUPS):
        cur = [jnp.where(picked[g] > 0.5, ninf, gs[g]) for g in range(N_GROUPS)]
        m = functools.reduce(jnp.maximum, cur)
        gi = jnp.full((1, tt), N_GROUPS, i32)
        for g in reversed(range(N_GROUPS)):
            gi = jnp.where(cur[g] == m, g, gi)
        picked = [jnp.where(gi == g, 1.0, picked[g]) for g in range(N_GROUPS)]

    for g in range(N_GROUPS):
        grows = pl.ds(g * gsz, gsz)
        allowed = jnp.broadcast_to(picked[g], (gsz, tt)) > 0.5
        cand_s[grows, :] = jnp.where(allowed, s_s[grows, :] + bias_ref[grows, :], ninf)
    cand = cand_s[...]
    s = s_s[...]

    eis, ws = [], []
    chosen = jnp.zeros((ne, tt), jnp.bool_)
    for _ in range(TOP_K):
        m = jnp.max(cand, axis=0, keepdims=True)
        ei = jnp.min(jnp.where(cand == m, row, ne), axis=0, keepdims=True)
        hit = row == ei
        ws.append(jnp.sum(jnp.where(hit, s, 0.0), axis=0, keepdims=True))
        cand = jnp.where(hit, ninf, cand)
        chosen = chosen | hit
        eis.append(ei)
    wsum = functools.reduce(jnp.add, ws)
    scale = ROUTE_SCALE / wsum

    onehot = chosen.astype(bf16)
    ti = lax.broadcasted_iota(i32, (tt, tt), 0)
    tj = lax.broadcasted_iota(i32, (tt, tt), 1)
    before = (ti < tj).astype(bf16)
    prior = jnp.dot(onehot, before, preferred_element_type=f32) + run_s[:, 0:1]
    ct = eidx_ref.shape[2]
    for k in range(TOP_K):
        hit = row == eis[k]
        rank = jnp.sum(jnp.where(hit, prior, 0.0), axis=0, keepdims=True).astype(i32)
        wsel_ref[k:k + 1, :] = ws[k] * scale
        for part in range(tt // ct):
            eidx_ref[part, k:k + 1, :] = eis[k][:, part * ct:(part + 1) * ct]
            rank_ref[part, k:k + 1, :] = rank[:, part * ct:(part + 1) * ct]
    run_s[...] = run_s[...] + jnp.sum(chosen.astype(f32), axis=1, keepdims=True)
    cnt_ref[...] = run_s[...].astype(i32)


def _route(lgt, b_router):
    ne, t = lgt.shape
    tt, ct = ROUTE_TILE, COMB_TILE
    tiled = pl.BlockSpec((tt // ct, TOP_K, ct), lambda i: (i, 0, 0))
    tiled_shape = jax.ShapeDtypeStruct((t // ct, TOP_K, ct), i32)
    return pl.pallas_call(
        _route_body,
        grid=(t // tt,),
        in_specs=[pl.BlockSpec((ne, tt), lambda i: (0, i)),
                  pl.BlockSpec((ne, 1), lambda i: (0, 0))],
        out_specs=[tiled,
                   pl.BlockSpec((TOP_K, tt), lambda i: (0, i)),
                   tiled,
                   pl.BlockSpec((ne, LANE), lambda i: (0, 0))],
        out_shape=[tiled_shape, jax.ShapeDtypeStruct((TOP_K, t), f32), tiled_shape,
                   jax.ShapeDtypeStruct((ne, LANE), i32)],
        scratch_shapes=[pltpu.VMEM((ne, LANE), f32), pltpu.VMEM((ne, tt), f32), pltpu.VMEM((ne, tt), f32)],
        compiler_params=_cparams("arbitrary"),
        name="route",
    )(lgt, b_router.reshape(ne, 1))


def _dispatch_body(eidx_ref, rank_ref, pstart_ref, padstart_ref, hf_ref, xs_ref, dest_ref, zero_s, sem, zsem):
    i = pl.program_id(0)
    tt = hf_ref.shape[0] // NSLAB
    ct = eidx_ref.shape[2]
    ne = padstart_ref.shape[0]
    nfill = zero_s.shape[0] // NSLAB

    @pl.when(i == 0)
    def _():
        zero_s[...] = jnp.zeros_like(zero_s)

        def fill(e, c):
            pltpu.make_async_copy(zero_s, _slab(xs_ref, padstart_ref[e], nfill), zsem).start()
            return c

        lax.fori_loop(0, ne, fill, 0)

        def drain(e, c):
            pltpu.make_async_copy(zero_s, _slab(xs_ref, 0, nfill), zsem).wait()
            return c

        lax.fori_loop(0, ne, drain, 0)

    for part in range(tt // ct):
        def send(t, c, part=part):
            for k in range(TOP_K):
                dest = pstart_ref[eidx_ref[part, k, t]] + rank_ref[part, k, t]
                dest_ref[part, k, t] = dest
                pltpu.make_async_copy(_slab(hf_ref, part * ct + t, 1), _slab(xs_ref, dest, 1), sem).start()
            return c

        lax.fori_loop(0, ct, send, 0)
    for k in range(TOP_K):
        pltpu.make_async_copy(hf_ref, _slab(xs_ref, 0, tt), sem).wait()


def _dispatch(eidx3, rank3, pstart, padstart, hfs, p_rows):
    t = hfs.shape[0] // NSLAB
    tt = ROUTE_TILE
    ct = eidx3.shape[2]
    idx = pl.BlockSpec((tt // ct, TOP_K, ct), lambda i: (i, 0, 0), memory_space=pltpu.SMEM)
    return pl.pallas_call(
        _dispatch_body,
        grid=(t // tt,),
        in_specs=[idx, idx,
                  pl.BlockSpec(memory_space=pltpu.SMEM),
                  pl.BlockSpec(memory_space=pltpu.SMEM),
                  pl.BlockSpec((tt * NSLAB, LANE), lambda i: (i, 0))],
        out_specs=[pl.BlockSpec(memory_space=pl.ANY), idx],
        out_shape=[jax.ShapeDtypeStruct(((p_rows + MOE_BLOCK) * NSLAB, LANE), hfs.dtype),
                   jax.ShapeDtypeStruct(eidx3.shape, i32)],
        scratch_shapes=[pltpu.VMEM((MOE_BLOCK * NSLAB, LANE), hfs.dtype), pltpu.SemaphoreType.DMA,
                        pltpu.SemaphoreType.DMA],
        compiler_params=_cparams("arbitrary"),
        name="dispatch",
    )(eidx3, rank3, pstart, padstart, hfs)


def _experts_body(bstart_ref, nblk_ref, nused_ref, xs_ref, wg_ref, wu_ref, wd_ref, eo_ref,
                  xbuf, obuf, stage_s, wgu_s, wd_s, isem, osem):
    e = pl.program_id(0)
    blk = stage_s.shape[0] // NSLAB
    f = wg_ref.shape[2]
    nused = nused_ref[0]

    def in_copy(g):
        slot = lax.rem(g, IN_SLOTS)
        return pltpu.make_async_copy(_slab(xs_ref, g * blk, blk), _slab(xbuf, slot * blk, blk), isem.at[slot])

    def out_copy(g):
        slot = lax.rem(g, OUT_SLOTS)
        return pltpu.make_async_copy(_slab(obuf, slot * blk, blk), _slab(eo_ref, g * blk, blk), osem.at[slot])

    @pl.when(e == 0)
    def _():
        for g in range(IN_SLOTS - 1):
            pl.when(g < nused)(lambda g=g: in_copy(jnp.int32(g)).start())

    @pl.when(nblk_ref[e] > 0)
    def _():
        wgu_s[:, 0:f] = wg_ref[0].astype(bf16)
        wgu_s[:, f:2 * f] = wu_ref[0].astype(bf16)
        wd_s[...] = wd_ref[0].astype(bf16)

        def block(g, c):
            in_copy(g).wait()

            @pl.when(g + IN_SLOTS - 1 < nused)
            def _():
                in_copy(g + IN_SLOTS - 1).start()

            islot = pl.multiple_of(lax.rem(g, IN_SLOTS) * (blk * NSLAB), blk * NSLAB)
            stage_s[...] = xbuf[pl.ds(islot, blk * NSLAB), :].astype(f32)
            x = _load_rows(stage_s, 0, blk).astype(bf16)
            gu = jnp.dot(x, wgu_s[...], preferred_element_type=f32)
            gate = gu[:, 0:f]
            h = (gate * jax.nn.sigmoid(gate) * gu[:, f:2 * f]).astype(bf16)
            o = jnp.dot(h, wd_s[...], preferred_element_type=f32)

            @pl.when(g >= OUT_SLOTS)
            def _():
                out_copy(g - OUT_SLOTS).wait()

            _store_rows(obuf, lax.rem(g, OUT_SLOTS) * blk, o)
            out_copy(g).start()
            return c

        lax.fori_loop(bstart_ref[e], bstart_ref[e] + nblk_ref[e], block, 0)

    @pl.when(e == pl.num_programs(0) - 1)
    def _():
        for back in range(1, OUT_SLOTS + 1):
            pl.when(nused >= back)(lambda back=back: out_copy(nused - back).wait())


def _experts(bstart, nblk, nused, xs, w_gate, w_up, w_down, nb):
    ne, d, f = w_gate.shape
    assert d == NSLAB * LANE
    blk = MOE_BLOCK
    w_map = lambda e, *_: (e, 0, 0)
    grid_spec = pltpu.PrefetchScalarGridSpec(
        num_scalar_prefetch=3,
        grid=(ne,),
        in_specs=[pl.BlockSpec(memory_space=pl.ANY),
                  pl.BlockSpec((1, d, f), w_map),
                  pl.BlockSpec((1, d, f), w_map),
                  pl.BlockSpec((1, f, d), w_map)],
        out_specs=pl.BlockSpec(memory_space=pl.ANY),
        scratch_shapes=[pltpu.VMEM((IN_SLOTS * blk * NSLAB, LANE), xs.dtype),
                        pltpu.VMEM((OUT_SLOTS * blk * NSLAB, LANE), f32),
                        pltpu.VMEM((blk * NSLAB, LANE), f32),
                        pltpu.VMEM((d, 2 * f), bf16), pltpu.VMEM((f, d), bf16),
                        pltpu.SemaphoreType.DMA((IN_SLOTS,)), pltpu.SemaphoreType.DMA((OUT_SLOTS,))],
    )
    return pl.pallas_call(
        _experts_body,
        grid_spec=grid_spec,
        out_shape=jax.ShapeDtypeStruct((nb * blk * NSLAB, LANE), f32),
        compiler_params=_cparams("arbitrary"),
        name="experts",
    )(bstart, nblk, nused, xs, w_gate, w_up, w_down)


def _combine_body(dest_ref, destn_ref, eo_ref, x1_ref, hp_ref, w_ref, mod_ref, wsg_ref, wsu_ref, wsd_ref,
                  nf_ref, y_ref, buf, stage_s, sem, *, chunk_seq):
    i = pl.program_id(0)
    tt, d = x1_ref.shape
    slot = i & 1

    def gather(idx_ref, into):
        def fetch(t, c):
            for k in range(TOP_K):
                pltpu.make_async_copy(_slab(eo_ref, idx_ref[0, k, t], 1),
                                      _slab(buf, (into * TOP_K + k) * tt + t, 1), sem.at[into]).start()
            return c

        lax.fori_loop(0, tt, fetch, 0)

    pl.when(i == 0)(lambda: gather(dest_ref, slot))
    pl.when(i + 1 < pl.num_programs(0))(lambda: gather(destn_ref, 1 - slot))

    stage_s[...] = hp_ref[...].astype(f32)
    hf = _load_rows(stage_s, 0, tt).astype(bf16)
    g = jnp.dot(hf, wsg_ref[...], preferred_element_type=f32)
    u = jnp.dot(hf, wsu_ref[...], preferred_element_type=f32)
    acc = jnp.dot((g * jax.nn.sigmoid(g) * u).astype(bf16), wsd_ref[...], preferred_element_type=f32)

    for k in range(TOP_K):
        pltpu.make_async_copy(_slab(eo_ref, 0, tt), _slab(buf, (slot * TOP_K + k) * tt, tt), sem.at[slot]).wait()
    for k in range(TOP_K):
        acc = acc + _load_rows(buf, (slot * TOP_K + k) * tt, tt) * w_ref[:, k:k + 1]

    for j in range(tt // CHUNK):
        rows = pl.ds(j * CHUNK, CHUNK)
        g2 = mod_ref[j if chunk_seq else 0][5:6, :]
        y = x1_ref[rows, :] + g2 * acc[j * CHUNK:(j + 1) * CHUNK, :]
        y_ref[rows, :] = _rms(y) * nf_ref[...]


def _combine(dest3, eo, x1, hp, wsel, mod, p, nseq, seq_len, tile_offset):
    d = x1.shape[1]
    tt = COMB_TILE
    f = p["ws_gate"].shape[1]
    t = nseq * seq_len
    chunk_seq = seq_len == CHUNK
    if chunk_seq:
        spt, tiles_per_seq = tt // CHUNK, 1
        seq_of = lambda i: i
    else:
        assert seq_len % tt == 0
        spt, tiles_per_seq = 1, seq_len // tt
        seq_of = lambda i: i // tiles_per_seq
    body = functools.partial(_combine_body, chunk_seq=chunk_seq)
    const = lambda shape: pl.BlockSpec(shape, lambda i: (0,) * len(shape))
    tok = lambda w: pl.BlockSpec((tt, w), lambda i: (i + tile_offset, 0))
    ntiles = t // tt
    cur = pl.BlockSpec((1, TOP_K, tt), lambda i: (i + tile_offset, 0, 0), memory_space=pltpu.SMEM)
    nxt = pl.BlockSpec((1, TOP_K, tt), lambda i: (jnp.minimum(i + 1, ntiles - 1) + tile_offset, 0, 0),
                       memory_space=pltpu.SMEM)
    return pl.pallas_call(
        body,
        grid=(ntiles,),
        in_specs=[cur, nxt,
                  pl.BlockSpec(memory_space=pl.ANY),
                  tok(d), pl.BlockSpec((tt * NSLAB, LANE), lambda i: (i + tile_offset, 0)), tok(TOP_K),
                  pl.BlockSpec((spt, 6, d), lambda i: (seq_of(i), 0, 0)),
                  const((d, f)), const((d, f)), const((f, d)), const((1, d))],
        out_specs=pl.BlockSpec((tt, d), lambda i: (i, 0)),
        out_shape=jax.ShapeDtypeStruct((t, d), f32),
        scratch_shapes=[pltpu.VMEM((2 * TOP_K * tt * NSLAB, LANE), eo.dtype),
                        pltpu.VMEM((tt * NSLAB, LANE), f32), pltpu.SemaphoreType.DMA((2,))],
        compiler_params=_cparams("arbitrary"),
        name="combine_sample" if chunk_seq else "combine_prompt",
    )(dest3, dest3, eo, x1, hp, wsel, mod, p["ws_gate"], p["ws_up"], p["ws_down"], p["norm_final"])


def kernel(x_prompt, x_sample, state_hgrn, state_gla, c_prompt, c_sample, w_ada, b_ada, norm1, norm2, w_in,
           hg_lb_logits, gla_gk_up, gla_gk_bias, hg_out_norm, gla_out_norm, w_out, w_router, b_router,
           w_gate, w_up, w_down, ws_gate, ws_up, ws_down, norm_final):
    nbp, lp, d = x_prompt.shape
    nbs, ls, _ = x_sample.shape
    depth = w_in.shape[0]
    assert depth == 1
    ne = w_router.shape[2]
    hgw = hg_out_norm.shape[1]
    gkw = gla_gk_up.shape[2]
    rank = gla_gk_up.shape[1]
    d_in = w_in.shape[2]

    nz = d_in - rank + LANE
    w_in_p = jnp.pad(w_in[0].astype(bf16), ((0, 0), (0, nz - d_in)))
    p = dict(
        norm1=norm1[0].reshape(1, d), norm2=norm2[0].reshape(1, d), w_in=w_in_p,
        lb_logits=hg_lb_logits,
        gk_up=jnp.stack(_split(jnp.pad(gla_gk_up[0], ((0, LANE - rank), (0, 0))))),
        gk_bias=gla_gk_bias[0].reshape(1, gkw),
        hg_norm=hg_out_norm[0].reshape(1, hgw), gla_norm=gla_out_norm[0].reshape(1, -1),
        w_out=w_out[0].astype(bf16), w_rt=jnp.stack(_split(w_router[0].T)),
        ws_gate=ws_gate[0].astype(bf16), ws_up=ws_up[0].astype(bf16), ws_down=ws_down[0].astype(bf16),
        norm_final=norm_final.reshape(1, d),
    )

    c_all = jnp.concatenate([c_prompt, c_sample], axis=0)
    mod = _ada(c_all, w_ada[0], b_ada[0]).reshape(nbp + nbs, 6, d)

    zero_hg = jnp.zeros((nbp,) + state_hgrn.shape[2:], f32)
    zero_gl = jnp.zeros((nbp,) + state_gla.shape[2:], f32)
    tp, ts = nbp * lp, nbs * ls
    t = tp + ts
    x1, hp, lgt, hg_p, gl_p = _mixer(x_prompt, mod[:nbp], zero_hg, zero_gl, p, False, t, 0)
    x1, hp, lgt, hg_s, gl_s = _mixer(x_sample, mod[nbp:], state_hgrn[0], state_gla[0], p, True, t,
                                     tp // MIX_TILE, prev=(x1, hp, lgt))

    eidx3, wsel_t, rank3, cnt = _route(lgt, b_router[0])

    counts = cnt[:, 0]
    padded = (counts + MOE_BLOCK - 1) // MOE_BLOCK * MOE_BLOCK
    pends = jnp.cumsum(padded)
    pstarts = pends - padded
    nb = -(-(t * TOP_K + ne * (MOE_BLOCK - 1)) // MOE_BLOCK)
    nused = (pends[-1] // MOE_BLOCK).astype(i32).reshape(1)

    pstarts = pstarts.astype(i32)
    xs, dest3 = _dispatch(eidx3, rank3, pstarts, (pstarts + counts).astype(i32), hp, nb * MOE_BLOCK)
    eo = _experts(pstarts // MOE_BLOCK, (padded // MOE_BLOCK).astype(i32), nused, xs,
                  w_gate[0], w_up[0], w_down[0], nb)
    wsel = wsel_t.T
    y_prompt = _combine(dest3, eo, x1, hp, wsel, mod[:nbp], p, nbp, lp, 0)
    y_sample = _combine(dest3, eo, x1, hp, wsel, mod[nbp:], p, nbs, ls, tp // COMB_TILE)
    return (y_prompt.reshape(nbp, lp, d), y_sample.reshape(nbs, ls, d),
            hg_p[None], gl_p[None], hg_s[None], gl_s[None])
```

```python
import functools

import jax
import jax.numpy as jnp
from jax import lax
from jax.experimental import pallas as pl
from jax.experimental.pallas import tpu as pltpu

f32 = jnp.float32
bf16 = jnp.bfloat16
i32 = jnp.int32
HIGHEST = lax.Precision.HIGHEST

EPS = 1e-6
CHUNK = 64
HG_HEADS = 4
GLA_HEADS = 4
GLA_GATE_NORMALIZER = 16.0
TOP_K = 8
N_GROUPS = 8
TOPK_GROUPS = 4
ROUTE_SCALE = 2.5

LANE = 128
SUBLANE = 8
V7X_VMEM_BYTES = 64 * 1024 * 1024
VMEM_LIMIT = V7X_VMEM_BYTES - 8 * 1024 * 1024

MIX_TILE = 512
ROUTE_TILE = 512
COMB_TILE = 256
MOE_BLOCK = 256
IN_SLOTS = 3
OUT_SLOTS = 2


def _cparams(*sem):
    return pltpu.CompilerParams(dimension_semantics=sem, vmem_limit_bytes=VMEM_LIMIT)


def _sigmoid_pair(x):
    e = jnp.exp(-jnp.abs(x))
    r = 1.0 / (1.0 + e)
    er = e * r
    pos = x >= 0
    return jnp.where(pos, r, er), jnp.where(pos, er, r)


def _rms(x):
    return x * lax.rsqrt(jnp.mean(x * x, axis=-1, keepdims=True) + EPS)


def _split(x):
    hi = x.astype(bf16)
    return hi, (x - hi.astype(f32)).astype(bf16)


def _dot_split(a, b_hi, b_lo, dims):
    a_hi, a_lo = _split(a)
    dg = functools.partial(lax.dot_general, dimension_numbers=dims, preferred_element_type=f32)
    return dg(a_hi, b_hi) + (dg(a_lo, b_hi) + dg(a_hi, b_lo))


NSLAB = SUBLANE


def _slab(ref, row0, n):
    return ref.at[pl.ds(pl.multiple_of(row0 * NSLAB, NSLAB), n * NSLAB), :]


def _load_rows(ref, row0, n):
    return jnp.concatenate([ref[pl.ds(row0 * NSLAB + s, n, stride=NSLAB), :] for s in range(NSLAB)], axis=1)


def _store_rows(ref, row0, x):
    for s in range(NSLAB):
        ref[pl.ds(row0 * NSLAB + s, x.shape[0], stride=NSLAB), :] = x[:, s * LANE:(s + 1) * LANE]


def _ada_body(c_ref, w_ref, b_ref, o_ref):
    c = c_ref[...]
    a = c * jax.nn.sigmoid(c)
    o_ref[...] = jnp.dot(a, w_ref[...], preferred_element_type=f32, precision=HIGHEST) + b_ref[...]


def _ada(c_all, w, b):
    ns, d = c_all.shape
    n = w.shape[1]
    tn = 1024
    return pl.pallas_call(
        _ada_body,
        grid=(n // tn,),
        in_specs=[pl.BlockSpec((ns, d), lambda j: (0, 0)),
                  pl.BlockSpec((d, tn), lambda j: (0, j)),
                  pl.BlockSpec((1, tn), lambda j: (0, j))],
        out_specs=pl.BlockSpec((ns, tn), lambda j: (0, j)),
        out_shape=jax.ShapeDtypeStruct((ns, n), f32),
        compiler_params=_cparams("arbitrary"),
        name="ada",
    )(c_all, w, b.reshape(1, n))


LEVELS = tuple(1 << n for n in range(CHUNK.bit_length() - 1))
NT = (((1,), (1,)), ((), ()))
TN = (((0,), (0,)), ((), ()))


def _pair_masks(c):
    t = lax.broadcasted_iota(i32, (c, c), 0)
    s = lax.broadcasted_iota(i32, (c, c), 1)
    x = t ^ s
    masks = [(x == 0).astype(f32)]
    for m in LEVELS:
        masks.append(((t > s) & (x >= m) & (x < 2 * m)).astype(f32))
    return masks


def _chunk_scan(q, k, g):
    c = q.shape[0]
    row = lax.broadcasted_iota(i32, q.shape, 0)
    pre = g
    tot = g
    zs = []
    for m in LEVELS:
        upper = (row & (2 * m - 1)) >= m
        e = jnp.exp(jnp.where(upper, pre, tot - pre))
        zs.append((jnp.where(upper, q, k) * e).astype(bf16))
        below = pltpu.roll(tot, m, 0)
        above = pltpu.roll(tot, c - m, 0)
        pre = pre + jnp.where(upper, below, 0.0)
        tot = tot + jnp.where(upper, below, above)
    qb = (q * jnp.exp(pre)).astype(bf16)
    kd = (k * jnp.exp(tot - pre)).astype(bf16)
    return zs, qb, kd, tot[0:1, :]


def _chunk_head(zs, qd, kd0, qb, kd, decay, vt, st_ref, slot, head, keep, lvl):
    own = (lambda a: a) if keep is None else (lambda a: a * keep)
    att = lax.dot_general(own(qd), kd0, NT, preferred_element_type=f32) * lvl[0]
    for z, mask in zip(zs, lvl[1:]):
        att = att + lax.dot_general(own(z), z, NT, preferred_element_type=f32) * mask
    st = st_ref[slot, head]
    lhs = jnp.concatenate([own(qb), att.astype(bf16)], axis=1)
    rhs_t = jnp.concatenate([st.astype(bf16), vt], axis=1)
    o = lax.dot_general(lhs, rhs_t, NT, preferred_element_type=f32)
    st_ref[slot, head] = st * decay + jnp.dot(vt, own(kd), preferred_element_type=f32)
    return o


def _mixer_body(*refs, nch, chunk_seq, tiles_per_seq, nprev):
    (x_ref, mod_ref, hg0_ref, gl0_ref, n1_ref, n2_ref, win_ref, lbl_ref, gup_ref, gbias_ref,
     hgn_ref, gln_ref, wout_ref, wrt_ref) = refs[:14]
    (x1_ref, hp_ref, lgt_ref, hg_ref, gl_ref, h_s, z_s, o_s, hf_s, slab_s, hgt_s, glt_s) = refs[14 + nprev:]
    i = pl.program_id(0)
    d = x_ref.shape[1]
    hdk, hdv = hg0_ref.shape[2], hg0_ref.shape[3]
    gdk, gdv = gl0_ref.shape[2], gl0_ref.shape[3]
    assert hdk == LANE and LANE % gdk == 0
    hgw = HG_HEADS * hdk
    gkw = GLA_HEADS * gdk
    gvw = GLA_HEADS * gdv
    nslot = hgt_s.shape[0]
    gla_lanes = lambda h: pl.ds((h * gdk) % LANE, gdk)

    def load_state():
        for s in range(nslot):
            for h in range(HG_HEADS):
                hgt_s[s, h] = hg0_ref[s, h].T
            for h in range(GLA_HEADS):
                glt_s[s, h] = jnp.zeros(glt_s.shape[2:], f32)
                glt_s[s, h, :, gla_lanes(h)] = gl0_ref[s, h].T

    if chunk_seq:
        load_state()
    else:
        pl.when(i % tiles_per_seq == 0)(load_state)

    lbl = lbl_ref[...]
    lbe = jnp.exp(lbl - jnp.max(lbl, axis=0, keepdims=True))
    lb = lbe[0:1, :] / jnp.sum(lbe, axis=0, keepdims=True)

    for j in range(nch):
        rows = pl.ds(j * CHUNK, CHUNK)
        m = mod_ref[j if chunk_seq else 0]
        xn = _rms(x_ref[rows, :]) * n1_ref[...]
        h_s[rows, :] = (xn * (1.0 + m[1:2, :]) + m[0:1, :]).astype(bf16)

    nz = z_s.shape[1]
    step = 512
    for n0 in range(0, nz, step):
        n1 = min(n0 + step, nz)
        z_s[:, n0:n1] = jnp.dot(h_s[...], win_ref[:, n0:n1], preferred_element_type=f32)

    o_hq, o_hf, o_hi, o_hgate = 0, hgw, 2 * hgw, 3 * hgw
    o_gq = 4 * hgw
    o_gk = o_gq + gkw
    o_gv = o_gk + gkw
    o_ggate = o_gv + gvw
    o_glr = o_ggate + gvw

    lvl = _pair_masks(CHUNK)
    lane = lax.broadcasted_iota(i32, (CHUNK, LANE), 1)
    heads_per_block = LANE // gdk
    for j in range(nch):
        rows = pl.ds(j * CHUNK, CHUNK)
        slot = j if chunk_seq else 0
        for h in range(HG_HEADS):
            lbh = lb[:, h * hdk:(h + 1) * hdk]
            hq = z_s[rows, pl.ds(o_hq + h * hdk, hdk)]
            sig, nsig = _sigmoid_pair(z_s[rows, pl.ds(o_hf + h * hdk, hdk)])
            g = jnp.log(lbh + (1.0 - lbh) * sig)
            k = (1.0 - lbh) * nsig
            q = hq * jax.nn.sigmoid(hq) * (hdk ** -0.5)
            zs, qb, kd, b_last = _chunk_scan(q, k, g)
            v = z_s[rows, pl.ds(o_hi + h * hdv, hdv)].T.astype(bf16)
            o = _chunk_head(zs, q.astype(bf16), k.astype(bf16), qb, kd, jnp.exp(b_last), v,
                            hgt_s, slot, h, None, lvl)
            hgate = z_s[rows, pl.ds(o_hgate + h * hdv, hdv)]
            o = _rms(o) * hgn_ref[:, pl.ds(h * hdv, hdv)] * jax.nn.sigmoid(hgate)
            o_s[rows, pl.ds(h * hdv, hdv)] = o.astype(bf16)
        u = _dot_split(z_s[rows, pl.ds(o_glr, LANE)], gup_ref[0], gup_ref[1], (((1,), (0,)), ((), ())))
        u = u + gbias_ref[...]
        loga = (jnp.minimum(u, 0.0) - jnp.log1p(jnp.exp(-jnp.abs(u)))) * (1.0 / GLA_GATE_NORMALIZER)
        for blk in range(GLA_HEADS // heads_per_block):
            q = z_s[rows, pl.ds(o_gq + blk * LANE, LANE)] * (gdk ** -0.5)
            k = z_s[rows, pl.ds(o_gk + blk * LANE, LANE)]
            zs, qb, kd, b_last = _chunk_scan(q, k, loga[:, blk * LANE:(blk + 1) * LANE])
            qd, kd0, decay = q.astype(bf16), k.astype(bf16), jnp.exp(b_last)
            for h in range(blk * heads_per_block, (blk + 1) * heads_per_block):
                lo = (h * gdk) % LANE
                keep = ((lane >= lo) & (lane < lo + gdk)).astype(bf16)
                v = z_s[rows, pl.ds(o_gv + h * gdv, gdv)].T.astype(bf16)
                o = _chunk_head(zs, qd, kd0, qb, kd, decay, v, glt_s, slot, h, keep, lvl)
                ggate = z_s[rows, pl.ds(o_ggate + h * gdv, gdv)]
                o = _rms(o) * gln_ref[...] * (ggate * jax.nn.sigmoid(ggate))
                o_s[rows, pl.ds(hgw + h * gdv, gdv)] = o.astype(bf16)

    a = jnp.dot(o_s[...], wout_ref[...], preferred_element_type=f32)
    for j in range(nch):
        rows = pl.ds(j * CHUNK, CHUNK)
        m = mod_ref[j if chunk_seq else 0]
        x1 = x_ref[rows, :] + m[2:3, :] * a[j * CHUNK:(j + 1) * CHUNK, :]
        x1_ref[rows, :] = x1
        hf = _rms(x1) * n2_ref[...] * (1.0 + m[4:5, :]) + m[3:4, :]
        hf_s[rows, :] = hf
        _store_rows(slab_s, j * CHUNK, hf)
    hp_ref[...] = slab_s[...].astype(bf16)
    h_hi, h_lo = _split(hf_s[...])
    nt = functools.partial(lax.dot_general, dimension_numbers=NT, preferred_element_type=f32)
    lgt_ref[...] = nt(wrt_ref[0], h_hi) + (nt(wrt_ref[1], h_hi) + nt(wrt_ref[0], h_lo))

    def store_state():
        for s in range(nslot):
            for h in range(HG_HEADS):
                hg_ref[s, h] = hgt_s[s, h].T
            for h in range(GLA_HEADS):
                gl_ref[s, h] = glt_s[s, h, :, gla_lanes(h)].T

    if chunk_seq:
        store_state()
    else:
        pl.when(i % tiles_per_seq == tiles_per_seq - 1)(store_state)


def _mixer(x, mod, hg0, gl0, p, chunk_seq, t_total, tile_offset, prev=()):
    nseq, L, d = x.shape
    t = nseq * L
    tm = MIX_TILE
    nch = tm // CHUNK
    if chunk_seq:
        assert L == CHUNK and nseq % nch == 0
        spt, tiles_per_seq = nch, 1
        seq_of = lambda i: i
    else:
        assert L % tm == 0
        spt, tiles_per_seq = 1, L // tm
        seq_of = lambda i: i // tiles_per_seq
    ne = p["w_rt"].shape[1]
    nz = p["w_in"].shape[1]
    _, hh, hdk, hdv = hg0.shape
    _, gh, gdk, gdv = gl0.shape
    const = lambda shape: pl.BlockSpec(shape, lambda i: (0,) * len(shape))
    nprev = len(prev)
    n_in = 14
    body = functools.partial(_mixer_body, nch=nch, chunk_seq=chunk_seq, tiles_per_seq=tiles_per_seq,
                             nprev=nprev)
    return pl.pallas_call(
        body,
        grid=(t // tm,),
        in_specs=[pl.BlockSpec((tm, d), lambda i: (i, 0)),
                  pl.BlockSpec((spt, 6, d), lambda i: (seq_of(i), 0, 0)),
                  pl.BlockSpec((spt, hh, hdk, hdv), lambda i: (seq_of(i), 0, 0, 0)),
                  pl.BlockSpec((spt, gh, gdk, gdv), lambda i: (seq_of(i), 0, 0, 0)),
                  const((1, d)), const((1, d)), const((d, nz)), const(p["lb_logits"].shape),
                  const(p["gk_up"].shape), const(p["gk_bias"].shape), const(p["hg_norm"].shape),
                  const(p["gla_norm"].shape), const((d, d)), const((2, ne, d))]
                 + [pl.BlockSpec(memory_space=pl.ANY)] * nprev,
        out_specs=[pl.BlockSpec((tm, d), lambda i: (i + tile_offset, 0)),
                   pl.BlockSpec((tm * NSLAB, LANE), lambda i: (i + tile_offset, 0)),
                   pl.BlockSpec((ne, tm), lambda i: (0, i + tile_offset)),
                   pl.BlockSpec((spt, hh, hdk, hdv), lambda i: (seq_of(i), 0, 0, 0)),
                   pl.BlockSpec((spt, gh, gdk, gdv), lambda i: (seq_of(i), 0, 0, 0))],
        out_shape=[jax.ShapeDtypeStruct((t_total, d), f32),
                   jax.ShapeDtypeStruct((t_total * NSLAB, LANE), bf16),
                   jax.ShapeDtypeStruct((ne, t_total), f32),
                   jax.ShapeDtypeStruct(hg0.shape, f32),
                   jax.ShapeDtypeStruct(gl0.shape, f32)],
        input_output_aliases={n_in + j: j for j in range(nprev)},
        scratch_shapes=[pltpu.VMEM((tm, d), bf16),
                        pltpu.VMEM((tm, nz), f32),
                        pltpu.VMEM((tm, d), bf16),
                        pltpu.VMEM((tm, d), f32),
                        pltpu.VMEM((tm * NSLAB, LANE), f32),
                        pltpu.VMEM((spt, hh, hdv, hdk), f32),
                        pltpu.VMEM((spt, gh, gdv, LANE), f32)],
        compiler_params=_cparams("arbitrary"),
        name="mixer_sample" if chunk_seq else "mixer_prompt",
    )(x.reshape(t, d), mod, hg0, gl0, p["norm1"], p["norm2"], p["w_in"], p["lb_logits"], p["gk_up"],
      p["gk_bias"], p["hg_norm"], p["gla_norm"], p["w_out"], p["w_rt"], *prev)


def _route_body(lg_ref, bias_ref, eidx_ref, wsel_ref, rank_ref, cnt_ref, run_s, s_s, cand_s):
    i = pl.program_id(0)
    ne, tt = lg_ref.shape
    gsz = ne // N_GROUPS
    ninf = -jnp.inf

    @pl.when(i == 0)
    def _():
        run_s[...] = jnp.zeros_like(run_s)

    row = lax.broadcasted_iota(i32, (ne, tt), 0)
    rg = lax.broadcasted_iota(i32, (gsz, tt), 0)

    gs = []
    for g in range(N_GROUPS):
        grows = pl.ds(g * gsz, gsz)
        sg = jax.nn.sigmoid(lg_ref[grows, :])
        s_s[grows, :] = sg
        blk = sg + bias_ref[grows, :]
        m1 = jnp.max(blk, axis=0, keepdims=True)
        first = jnp.min(jnp.where(blk == m1, rg, gsz), axis=0, keepdims=True)
        m2 = jnp.max(jnp.where(rg == first, ninf, blk), axis=0, keepdims=True)
        gs.append(m1 + m2)

    picked = [jnp.zeros((1, tt), f32) for _ in range(N_GROUPS)]
    for _ in range(TOPK_GROUPS):
        cur = [jnp.where(picked[g] > 0.5, ninf, gs[g]) for g in range(N_GROUPS)]
        m = functools.reduce(jnp.maximum, cur)
        gi = jnp.full((1, tt), N_GROUPS, i32)
        for g in reversed(range(N_GROUPS)):
            gi = jnp.where(cur[g] == m, g, gi)
        picked = [jnp.where(gi == g, 1.0, picked[g]) for g in range(N_GROUPS)]

    for g in range(N_GROUPS):
        grows = pl.ds(g * gsz, gsz)
        allowed = jnp.broadcast_to(picked[g], (gsz, tt)) > 0.5
        cand_s[grows, :] = jnp.where(allowed, s_s[grows, :] + bias_ref[grows, :], ninf)
    cand = cand_s[...]
    s = s_s[...]

    eis, ws = [], []
    chosen = jnp.zeros((ne, tt), jnp.bool_)
    for _ in range(TOP_K):
        m = jnp.max(cand, axis=0, keepdims=True)
        ei = jnp.min(jnp.where(cand == m, row, ne), axis=0, keepdims=True)
        hit = row == ei
        ws.append(jnp.sum(jnp.where(hit, s, 0.0), axis=0, keepdims=True))
        cand = jnp.where(hit, ninf, cand)
        chosen = chosen | hit
        eis.append(ei)
    wsum = functools.reduce(jnp.add, ws)
    scale = ROUTE_SCALE / wsum

    onehot = chosen.astype(bf16)
    ti = lax.broadcasted_iota(i32, (tt, tt), 0)
    tj = lax.broadcasted_iota(i32, (tt, tt), 1)
    before = (ti < tj).astype(bf16)
    prior = jnp.dot(onehot, before, preferred_element_type=f32) + run_s[:, 0:1]
    ct = eidx_ref.shape[2]
    for k in range(TOP_K):
        hit = row == eis[k]
        rank = jnp.sum(jnp.where(hit, prior, 0.0), axis=0, keepdims=True).astype(i32)
        wsel_ref[k:k + 1, :] = ws[k] * scale
        for part in range(tt // ct):
            eidx_ref[part, k:k + 1, :] = eis[k][:, part * ct:(part + 1) * ct]
            rank_ref[part, k:k + 1, :] = rank[:, part * ct:(part + 1) * ct]
    run_s[...] = run_s[...] + jnp.sum(chosen.astype(f32), axis=1, keepdims=True)
    cnt_ref[...] = run_s[...].astype(i32)


def _route(lgt, b_router):
    ne, t = lgt.shape
    tt, ct = ROUTE_TILE, COMB_TILE
    tiled = pl.BlockSpec((tt // ct, TOP_K, ct), lambda i: (i, 0, 0))
    tiled_shape = jax.ShapeDtypeStruct((t // ct, TOP_K, ct), i32)
    return pl.pallas_call(
        _route_body,
        grid=(t // tt,),
        in_specs=[pl.BlockSpec((ne, tt), lambda i: (0, i)),
                  pl.BlockSpec((ne, 1), lambda i: (0, 0))],
        out_specs=[tiled,
                   pl.BlockSpec((TOP_K, tt), lambda i: (0, i)),
                   tiled,
                   pl.BlockSpec((ne, LANE), lambda i: (0, 0))],
        out_shape=[tiled_shape, jax.ShapeDtypeStruct((TOP_K, t), f32), tiled_shape,
                   jax.ShapeDtypeStruct((ne, LANE), i32)],
        scratch_shapes=[pltpu.VMEM((ne, LANE), f32), pltpu.VMEM((ne, tt), f32), pltpu.VMEM((ne, tt), f32)],
        compiler_params=_cparams("arbitrary"),
        name="route",
    )(lgt, b_router.reshape(ne, 1))


def _dest_body(eidx_ref, rank_ref, pstart_ref, dest_ref):
    ne = pstart_ref.shape[0]
    ct = eidx_ref.shape[2]
    row = lax.broadcasted_iota(i32, (ne, ct), 0)
    ps = pstart_ref[...]
    for k in range(TOP_K):
        hit = row == eidx_ref[0, k:k + 1, :]
        base = jnp.sum(jnp.where(hit, ps, 0.0), axis=0, keepdims=True)
        dest_ref[0, k:k + 1, :] = base.astype(i32) + rank_ref[0, k:k + 1, :]


def _dest(eidx3, rank3, pstarts):
    ntile, _, ct = eidx3.shape
    ne = pstarts.shape[0]
    tile = pl.BlockSpec((1, TOP_K, ct), lambda i: (i, 0, 0))
    return pl.pallas_call(
        _dest_body,
        grid=(ntile,),
        in_specs=[tile, tile, pl.BlockSpec((ne, 1), lambda i: (0, 0))],
        out_specs=tile,
        out_shape=jax.ShapeDtypeStruct(eidx3.shape, i32),
        compiler_params=_cparams("arbitrary"),
        name="dest",
    )(eidx3, rank3, pstarts.astype(f32).reshape(ne, 1))


def _dispatch_body(dest_ref, padstart_ref, hf_ref, xs_ref, zero_s, sem, zsem):
    i = pl.program_id(0)
    tt = hf_ref.shape[0] // NSLAB
    ct = dest_ref.shape[2]
    ne = padstart_ref.shape[0]
    nfill = zero_s.shape[0] // NSLAB

    @pl.when(i == 0)
    def _():
        zero_s[...] = jnp.zeros_like(zero_s)

        def fill(e, c):
            pltpu.make_async_copy(zero_s, _slab(xs_ref, padstart_ref[e], nfill), zsem).start()
            return c

        lax.fori_loop(0, ne, fill, 0)

        def drain(e, c):
            pltpu.make_async_copy(zero_s, _slab(xs_ref, 0, nfill), zsem).wait()
            return c

        lax.fori_loop(0, ne, drain, 0)

    for part in range(tt // ct):
        def send(t, c, part=part):
            for k in range(TOP_K):
                pltpu.make_async_copy(_slab(hf_ref, part * ct + t, 1),
                                      _slab(xs_ref, dest_ref[part, k, t], 1), sem).start()
            return c

        lax.fori_loop(0, ct, send, 0)
    for k in range(TOP_K):
        pltpu.make_async_copy(hf_ref, _slab(xs_ref, 0, tt), sem).wait()


def _dispatch(dest3, padstart, hfs, p_rows):
    t = hfs.shape[0] // NSLAB
    tt = ROUTE_TILE
    ct = dest3.shape[2]
    return pl.pallas_call(
        _dispatch_body,
        grid=(t // tt,),
        in_specs=[pl.BlockSpec((tt // ct, TOP_K, ct), lambda i: (i, 0, 0), memory_space=pltpu.SMEM),
                  pl.BlockSpec(memory_space=pltpu.SMEM),
                  pl.BlockSpec((tt * NSLAB, LANE), lambda i: (i, 0))],
        out_specs=pl.BlockSpec(memory_space=pl.ANY),
        out_shape=jax.ShapeDtypeStruct(((p_rows + MOE_BLOCK) * NSLAB, LANE), hfs.dtype),
        scratch_shapes=[pltpu.VMEM((MOE_BLOCK * NSLAB, LANE), hfs.dtype), pltpu.SemaphoreType.DMA,
                        pltpu.SemaphoreType.DMA],
        compiler_params=_cparams("arbitrary"),
        name="dispatch",
    )(dest3, padstart, hfs)


def _experts_body(bstart_ref, nblk_ref, nused_ref, xs_ref, wg_ref, wu_ref, wd_ref, eo_ref,
                  xbuf, obuf, stage_s, wgu_s, wd_s, isem, osem):
    e = pl.program_id(0)
    blk = stage_s.shape[0] // NSLAB
    f = wg_ref.shape[2]
    nused = nused_ref[0]

    def in_copy(g):
        slot = lax.rem(g, IN_SLOTS)
        return pltpu.make_async_copy(_slab(xs_ref, g * blk, blk), _slab(xbuf, slot * blk, blk), isem.at[slot])

    def out_copy(g):
        slot = lax.rem(g, OUT_SLOTS)
        return pltpu.make_async_copy(_slab(obuf, slot * blk, blk), _slab(eo_ref, g * blk, blk), osem.at[slot])

    @pl.when(e == 0)
    def _():
        for g in range(IN_SLOTS - 1):
            pl.when(g < nused)(lambda g=g: in_copy(jnp.int32(g)).start())

    @pl.when(nblk_ref[e] > 0)
    def _():
        wgu_s[:, 0:f] = wg_ref[0].astype(bf16)
        wgu_s[:, f:2 * f] = wu_ref[0].astype(bf16)
        wd_s[...] = wd_ref[0].astype(bf16)

        def block(g, c):
            in_copy(g).wait()

            @pl.when(g + IN_SLOTS - 1 < nused)
            def _():
                in_copy(g + IN_SLOTS - 1).start()

            islot = pl.multiple_of(lax.rem(g, IN_SLOTS) * (blk * NSLAB), blk * NSLAB)
            stage_s[...] = xbuf[pl.ds(islot, blk * NSLAB), :].astype(f32)
            x = _load_rows(stage_s, 0, blk).astype(bf16)
            gu = jnp.dot(x, wgu_s[...], preferred_element_type=f32)
            gate = gu[:, 0:f]
            h = (gate * jax.nn.sigmoid(gate) * gu[:, f:2 * f]).astype(bf16)
            o = jnp.dot(h, wd_s[...], preferred_element_type=f32)

            @pl.when(g >= OUT_SLOTS)
            def _():
                out_copy(g - OUT_SLOTS).wait()

            _store_rows(obuf, lax.rem(g, OUT_SLOTS) * blk, o)
            out_copy(g).start()
            return c

        lax.fori_loop(bstart_ref[e], bstart_ref[e] + nblk_ref[e], block, 0)

    @pl.when(e == pl.num_programs(0) - 1)
    def _():
        for back in range(1, OUT_SLOTS + 1):
            pl.when(nused >= back)(lambda back=back: out_copy(nused - back).wait())


def _experts(bstart, nblk, nused, xs, w_gate, w_up, w_down, nb):
    ne, d, f = w_gate.shape
    assert d == NSLAB * LANE
    blk = MOE_BLOCK
    w_map = lambda e, *_: (e, 0, 0)
    grid_spec = pltpu.PrefetchScalarGridSpec(
        num_scalar_prefetch=3,
        grid=(ne,),
        in_specs=[pl.BlockSpec(memory_space=pl.ANY),
                  pl.BlockSpec((1, d, f), w_map),
                  pl.BlockSpec((1, d, f), w_map),
                  pl.BlockSpec((1, f, d), w_map)],
        out_specs=pl.BlockSpec(memory_space=pl.ANY),
        scratch_shapes=[pltpu.VMEM((IN_SLOTS * blk * NSLAB, LANE), xs.dtype),
                        pltpu.VMEM((OUT_SLOTS * blk * NSLAB, LANE), f32),
                        pltpu.VMEM((blk * NSLAB, LANE), f32),
                        pltpu.VMEM((d, 2 * f), bf16), pltpu.VMEM((f, d), bf16),
                        pltpu.SemaphoreType.DMA((IN_SLOTS,)), pltpu.SemaphoreType.DMA((OUT_SLOTS,))],
    )
    return pl.pallas_call(
        _experts_body,
        grid_spec=grid_spec,
        out_shape=jax.ShapeDtypeStruct((nb * blk * NSLAB, LANE), f32),
        compiler_params=_cparams("arbitrary"),
        name="experts",
    )(bstart, nblk, nused, xs, w_gate, w_up, w_down)


def _combine_body(dest_ref, destn_ref, eo_ref, x1_ref, hp_ref, w_ref, mod_ref, wsg_ref, wsu_ref, wsd_ref,
                  nf_ref, y_ref, buf, stage_s, sem, *, chunk_seq):
    i = pl.program_id(0)
    tt, d = x1_ref.shape
    slot = i & 1

    def gather(idx_ref, into):
        def fetch(t, c):
            for k in range(TOP_K):
                pltpu.make_async_copy(_slab(eo_ref, idx_ref[0, k, t], 1),
                                      _slab(buf, (into * TOP_K + k) * tt + t, 1), sem.at[into]).start()
            return c

        lax.fori_loop(0, tt, fetch, 0)

    pl.when(i == 0)(lambda: gather(dest_ref, slot))
    pl.when(i + 1 < pl.num_programs(0))(lambda: gather(destn_ref, 1 - slot))

    stage_s[...] = hp_ref[...].astype(f32)
    hf = _load_rows(stage_s, 0, tt).astype(bf16)
    g = jnp.dot(hf, wsg_ref[...], preferred_element_type=f32)
    u = jnp.dot(hf, wsu_ref[...], preferred_element_type=f32)
    acc = jnp.dot((g * jax.nn.sigmoid(g) * u).astype(bf16), wsd_ref[...], preferred_element_type=f32)

    for k in range(TOP_K):
        pltpu.make_async_copy(_slab(eo_ref, 0, tt), _slab(buf, (slot * TOP_K + k) * tt, tt), sem.at[slot]).wait()
    for k in range(TOP_K):
        acc = acc + _load_rows(buf, (slot * TOP_K + k) * tt, tt) * w_ref[:, k:k + 1]

    for j in range(tt // CHUNK):
        rows = pl.ds(j * CHUNK, CHUNK)
        g2 = mod_ref[j if chunk_seq else 0][5:6, :]
        y = x1_ref[rows, :] + g2 * acc[j * CHUNK:(j + 1) * CHUNK, :]
        y_ref[rows, :] = _rms(y) * nf_ref[...]


def _combine(dest3, eo, x1, hp, wsel, mod, p, nseq, seq_len, tile_offset):
    d = x1.shape[1]
    tt = COMB_TILE
    f = p["ws_gate"].shape[1]
    t = nseq * seq_len
    chunk_seq = seq_len == CHUNK
    if chunk_seq:
        spt, tiles_per_seq = tt // CHUNK, 1
        seq_of = lambda i: i
    else:
        assert seq_len % tt == 0
        spt, tiles_per_seq = 1, seq_len // tt
        seq_of = lambda i: i // tiles_per_seq
    body = functools.partial(_combine_body, chunk_seq=chunk_seq)
    const = lambda shape: pl.BlockSpec(shape, lambda i: (0,) * len(shape))
    tok = lambda w: pl.BlockSpec((tt, w), lambda i: (i + tile_offset, 0))
    ntiles = t // tt
    cur = pl.BlockSpec((1, TOP_K, tt), lambda i: (i + tile_offset, 0, 0), memory_space=pltpu.SMEM)
    nxt = pl.BlockSpec((1, TOP_K, tt), lambda i: (jnp.minimum(i + 1, ntiles - 1) + tile_offset, 0, 0),
                       memory_space=pltpu.SMEM)
    return pl.pallas_call(
        body,
        grid=(ntiles,),
        in_specs=[cur, nxt,
                  pl.BlockSpec(memory_space=pl.ANY),
                  tok(d), pl.BlockSpec((tt * NSLAB, LANE), lambda i: (i + tile_offset, 0)), tok(TOP_K),
                  pl.BlockSpec((spt, 6, d), lambda i: (seq_of(i), 0, 0)),
                  const((d, f)), const((d, f)), const((f, d)), const((1, d))],
        out_specs=pl.BlockSpec((tt, d), lambda i: (i, 0)),
        out_shape=jax.ShapeDtypeStruct((t, d), f32),
        scratch_shapes=[pltpu.VMEM((2 * TOP_K * tt * NSLAB, LANE), eo.dtype),
                        pltpu.VMEM((tt * NSLAB, LANE), f32), pltpu.SemaphoreType.DMA((2,))],
        compiler_params=_cparams("arbitrary"),
        name="combine_sample" if chunk_seq else "combine_prompt",
    )(dest3, dest3, eo, x1, hp, wsel, mod, p["ws_gate"], p["ws_up"], p["ws_down"], p["norm_final"])


def kernel(x_prompt, x_sample, state_hgrn, state_gla, c_prompt, c_sample, w_ada, b_ada, norm1, norm2, w_in,
           hg_lb_logits, gla_gk_up, gla_gk_bias, hg_out_norm, gla_out_norm, w_out, w_router, b_router,
           w_gate, w_up, w_down, ws_gate, ws_up, ws_down, norm_final):
    nbp, lp, d = x_prompt.shape
    nbs, ls, _ = x_sample.shape
    depth = w_in.shape[0]
    assert depth == 1
    ne = w_router.shape[2]
    hgw = hg_out_norm.shape[1]
    gkw = gla_gk_up.shape[2]
    rank = gla_gk_up.shape[1]
    d_in = w_in.shape[2]

    nz = d_in - rank + LANE
    w_in_p = jnp.pad(w_in[0].astype(bf16), ((0, 0), (0, nz - d_in)))
    p = dict(
        norm1=norm1[0].reshape(1, d), norm2=norm2[0].reshape(1, d), w_in=w_in_p,
        lb_logits=hg_lb_logits,
        gk_up=jnp.stack(_split(jnp.pad(gla_gk_up[0], ((0, LANE - rank), (0, 0))))),
        gk_bias=gla_gk_bias[0].reshape(1, gkw),
        hg_norm=hg_out_norm[0].reshape(1, hgw), gla_norm=gla_out_norm[0].reshape(1, -1),
        w_out=w_out[0].astype(bf16), w_rt=jnp.stack(_split(w_router[0].T)),
        ws_gate=ws_gate[0].astype(bf16), ws_up=ws_up[0].astype(bf16), ws_down=ws_down[0].astype(bf16),
        norm_final=norm_final.reshape(1, d),
    )

    c_all = jnp.concatenate([c_prompt, c_sample], axis=0)
    mod = _ada(c_all, w_ada[0], b_ada[0]).reshape(nbp + nbs, 6, d)

    zero_hg = jnp.zeros((nbp,) + state_hgrn.shape[2:], f32)
    zero_gl = jnp.zeros((nbp,) + state_gla.shape[2:], f32)
    tp, ts = nbp * lp, nbs * ls
    t = tp + ts
    x1, hp, lgt, hg_p, gl_p = _mixer(x_prompt, mod[:nbp], zero_hg, zero_gl, p, False, t, 0)
    x1, hp, lgt, hg_s, gl_s = _mixer(x_sample, mod[nbp:], state_hgrn[0], state_gla[0], p, True, t,
                                     tp // MIX_TILE, prev=(x1, hp, lgt))

    eidx3, wsel_t, rank3, cnt = _route(lgt, b_router[0])

    counts = cnt[:, 0]
    padded = (counts + MOE_BLOCK - 1) // MOE_BLOCK * MOE_BLOCK
    pends = jnp.cumsum(padded)
    pstarts = pends - padded
    nb = -(-(t * TOP_K + ne * (MOE_BLOCK - 1)) // MOE_BLOCK)
    nused = (pends[-1] // MOE_BLOCK).astype(i32).reshape(1)

    pstarts = pstarts.astype(i32)
    dest3 = _dest(eidx3, rank3, pstarts)
    xs = _dispatch(dest3, (pstarts + counts).astype(i32), hp, nb * MOE_BLOCK)
    eo = _experts(pstarts // MOE_BLOCK, (padded // MOE_BLOCK).astype(i32), nused, xs,
                  w_gate[0], w_up[0], w_down[0], nb)
    wsel = wsel_t.T
    y_prompt = _combine(dest3, eo, x1, hp, wsel, mod[:nbp], p, nbp, lp, 0)
    y_sample = _combine(dest3, eo, x1, hp, wsel, mod[nbp:], p, nbs, ls, tp // COMB_TILE)
    return (y_prompt.reshape(nbp, lp, d), y_sample.reshape(nbs, ls, d),
            hg_p[None], gl_p[None], hg_s[None], gl_s[None])
```

```python
import functools

import jax
import jax.numpy as jnp
from jax import lax
from jax.experimental import pallas as pl
from jax.experimental.pallas import tpu as pltpu

f32 = jnp.float32
bf16 = jnp.bfloat16
i32 = jnp.int32
HIGHEST = lax.Precision.HIGHEST

EPS = 1e-6
CHUNK = 64
HG_HEADS = 4
GLA_HEADS = 4
GLA_GATE_NORMALIZER = 16.0
TOP_K = 8
N_GROUPS = 8
TOPK_GROUPS = 4
ROUTE_SCALE = 2.5

LANE = 128
SUBLANE = 8
V7X_VMEM_BYTES = 64 * 1024 * 1024
VMEM_LIMIT = V7X_VMEM_BYTES - 8 * 1024 * 1024

MIX_TILE = 512
ROUTE_TILE = 512
COMB_TILE = 256
MOE_BLOCK = 256
IN_SLOTS = 3
DMA_PARTS = 4
OUT_SLOTS = 2


def _cparams(*sem):
    return pltpu.CompilerParams(dimension_semantics=sem, vmem_limit_bytes=VMEM_LIMIT)


def _sigmoid_pair(x):
    e = jnp.exp(-jnp.abs(x))
    r = 1.0 / (1.0 + e)
    er = e * r
    pos = x >= 0
    return jnp.where(pos, r, er), jnp.where(pos, er, r)


def _rms(x):
    return x * lax.rsqrt(jnp.mean(x * x, axis=-1, keepdims=True) + EPS)


def _split(x):
    hi = x.astype(bf16)
    return hi, (x - hi.astype(f32)).astype(bf16)


def _dot_split(a, b_hi, b_lo, dims):
    a_hi, a_lo = _split(a)
    dg = functools.partial(lax.dot_general, dimension_numbers=dims, preferred_element_type=f32)
    return dg(a_hi, b_hi) + (dg(a_lo, b_hi) + dg(a_hi, b_lo))


NSLAB = SUBLANE


def _slab(ref, row0, n):
    return ref.at[pl.ds(pl.multiple_of(row0 * NSLAB, NSLAB), n * NSLAB), :]


def _load_rows(ref, row0, n):
    return jnp.concatenate([ref[pl.ds(row0 * NSLAB + s, n, stride=NSLAB), :] for s in range(NSLAB)], axis=1)


def _store_rows(ref, row0, x):
    for s in range(NSLAB):
        ref[pl.ds(row0 * NSLAB + s, x.shape[0], stride=NSLAB), :] = x[:, s * LANE:(s + 1) * LANE]


def _ada_body(c_ref, w_ref, b_ref, o_ref):
    c = c_ref[...]
    a = c * jax.nn.sigmoid(c)
    o_ref[...] = jnp.dot(a, w_ref[...], preferred_element_type=f32, precision=HIGHEST) + b_ref[...]


def _ada(c_all, w, b):
    ns, d = c_all.shape
    n = w.shape[1]
    tn = 1024
    return pl.pallas_call(
        _ada_body,
        grid=(n // tn,),
        in_specs=[pl.BlockSpec((ns, d), lambda j: (0, 0)),
                  pl.BlockSpec((d, tn), lambda j: (0, j)),
                  pl.BlockSpec((1, tn), lambda j: (0, j))],
        out_specs=pl.BlockSpec((ns, tn), lambda j: (0, j)),
        out_shape=jax.ShapeDtypeStruct((ns, n), f32),
        compiler_params=_cparams("arbitrary"),
        name="ada",
    )(c_all, w, b.reshape(1, n))


LEVELS = tuple(1 << n for n in range(CHUNK.bit_length() - 1))
NT = (((1,), (1,)), ((), ()))
TN = (((0,), (0,)), ((), ()))


def _pair_masks(c):
    t = lax.broadcasted_iota(i32, (c, c), 0)
    s = lax.broadcasted_iota(i32, (c, c), 1)
    x = t ^ s
    masks = [(x == 0).astype(f32)]
    for m in LEVELS:
        masks.append(((t > s) & (x >= m) & (x < 2 * m)).astype(f32))
    return masks


def _chunk_scan(q, k, g):
    c = q.shape[0]
    row = lax.broadcasted_iota(i32, q.shape, 0)
    pre = g
    tot = g
    zs = []
    for m in LEVELS:
        upper = (row & (2 * m - 1)) >= m
        e = jnp.exp(jnp.where(upper, pre, tot - pre))
        zs.append((jnp.where(upper, q, k) * e).astype(bf16))
        below = pltpu.roll(tot, m, 0)
        above = pltpu.roll(tot, c - m, 0)
        pre = pre + jnp.where(upper, below, 0.0)
        tot = tot + jnp.where(upper, below, above)
    qb = (q * jnp.exp(pre)).astype(bf16)
    kd = (k * jnp.exp(tot - pre)).astype(bf16)
    return zs, qb, kd, tot[0:1, :]


def _chunk_head(zs, qd, kd0, qb, kd, decay, vt, st_ref, slot, head, keep, lvl):
    own = (lambda a: a) if keep is None else (lambda a: a * keep)
    att = lax.dot_general(own(qd), kd0, NT, preferred_element_type=f32) * lvl[0]
    for z, mask in zip(zs, lvl[1:]):
        att = att + lax.dot_general(own(z), z, NT, preferred_element_type=f32) * mask
    st = st_ref[slot, head]
    lhs = jnp.concatenate([own(qb), att.astype(bf16)], axis=1)
    rhs_t = jnp.concatenate([st.astype(bf16), vt], axis=1)
    o = lax.dot_general(lhs, rhs_t, NT, preferred_element_type=f32)
    st_ref[slot, head] = st * decay + jnp.dot(vt, own(kd), preferred_element_type=f32)
    return o


def _mixer_body(*refs, nch, chunk_seq, tiles_per_seq, nprev):
    (x_ref, mod_ref, hg0_ref, gl0_ref, n1_ref, n2_ref, win_ref, lbl_ref, gup_ref, gbias_ref,
     hgn_ref, gln_ref, wout_ref, wrt_ref) = refs[:14]
    (x1_ref, hp_ref, lgt_ref, hg_ref, gl_ref, h_s, z_s, o_s, hf_s, slab_s, hgt_s, glt_s) = refs[14 + nprev:]
    i = pl.program_id(0)
    d = x_ref.shape[1]
    hdk, hdv = hg0_ref.shape[2], hg0_ref.shape[3]
    gdk, gdv = gl0_ref.shape[2], gl0_ref.shape[3]
    assert hdk == LANE and LANE % gdk == 0
    hgw = HG_HEADS * hdk
    gkw = GLA_HEADS * gdk
    gvw = GLA_HEADS * gdv
    nslot = hgt_s.shape[0]
    gla_lanes = lambda h: pl.ds((h * gdk) % LANE, gdk)

    def load_state():
        for s in range(nslot):
            for h in range(HG_HEADS):
                hgt_s[s, h] = hg0_ref[s, h].T
            for h in range(GLA_HEADS):
                glt_s[s, h] = jnp.zeros(glt_s.shape[2:], f32)
                glt_s[s, h, :, gla_lanes(h)] = gl0_ref[s, h].T

    if chunk_seq:
        load_state()
    else:
        pl.when(i % tiles_per_seq == 0)(load_state)

    lbl = lbl_ref[...]
    lbe = jnp.exp(lbl - jnp.max(lbl, axis=0, keepdims=True))
    lb = lbe[0:1, :] / jnp.sum(lbe, axis=0, keepdims=True)

    for j in range(nch):
        rows = pl.ds(j * CHUNK, CHUNK)
        m = mod_ref[j if chunk_seq else 0]
        xn = _rms(x_ref[rows, :]) * n1_ref[...]
        h_s[rows, :] = (xn * (1.0 + m[1:2, :]) + m[0:1, :]).astype(bf16)

    nz = z_s.shape[1]
    step = 512
    for n0 in range(0, nz, step):
        n1 = min(n0 + step, nz)
        z_s[:, n0:n1] = jnp.dot(h_s[...], win_ref[:, n0:n1], preferred_element_type=f32)

    o_hq, o_hf, o_hi, o_hgate = 0, hgw, 2 * hgw, 3 * hgw
    o_gq = 4 * hgw
    o_gk = o_gq + gkw
    o_gv = o_gk + gkw
    o_ggate = o_gv + gvw
    o_glr = o_ggate + gvw

    lvl = _pair_masks(CHUNK)
    lane = lax.broadcasted_iota(i32, (CHUNK, LANE), 1)
    heads_per_block = LANE // gdk
    for j in range(nch):
        rows = pl.ds(j * CHUNK, CHUNK)
        slot = j if chunk_seq else 0
        for h in range(HG_HEADS):
            lbh = lb[:, h * hdk:(h + 1) * hdk]
            hq = z_s[rows, pl.ds(o_hq + h * hdk, hdk)]
            sig, nsig = _sigmoid_pair(z_s[rows, pl.ds(o_hf + h * hdk, hdk)])
            g = jnp.log(lbh + (1.0 - lbh) * sig)
            k = (1.0 - lbh) * nsig
            q = hq * jax.nn.sigmoid(hq) * (hdk ** -0.5)
            zs, qb, kd, b_last = _chunk_scan(q, k, g)
            v = z_s[rows, pl.ds(o_hi + h * hdv, hdv)].T.astype(bf16)
            o = _chunk_head(zs, q.astype(bf16), k.astype(bf16), qb, kd, jnp.exp(b_last), v,
                            hgt_s, slot, h, None, lvl)
            hgate = z_s[rows, pl.ds(o_hgate + h * hdv, hdv)]
            o = _rms(o) * hgn_ref[:, pl.ds(h * hdv, hdv)] * jax.nn.sigmoid(hgate)
            o_s[rows, pl.ds(h * hdv, hdv)] = o.astype(bf16)
        u = _dot_split(z_s[rows, pl.ds(o_glr, LANE)], gup_ref[0], gup_ref[1], (((1,), (0,)), ((), ())))
        u = u + gbias_ref[...]
        loga = (jnp.minimum(u, 0.0) - jnp.log1p(jnp.exp(-jnp.abs(u)))) * (1.0 / GLA_GATE_NORMALIZER)
        for blk in range(GLA_HEADS // heads_per_block):
            q = z_s[rows, pl.ds(o_gq + blk * LANE, LANE)] * (gdk ** -0.5)
            k = z_s[rows, pl.ds(o_gk + blk * LANE, LANE)]
            zs, qb, kd, b_last = _chunk_scan(q, k, loga[:, blk * LANE:(blk + 1) * LANE])
            qd, kd0, decay = q.astype(bf16), k.astype(bf16), jnp.exp(b_last)
            for h in range(blk * heads_per_block, (blk + 1) * heads_per_block):
                lo = (h * gdk) % LANE
                keep = ((lane >= lo) & (lane < lo + gdk)).astype(bf16)
                v = z_s[rows, pl.ds(o_gv + h * gdv, gdv)].T.astype(bf16)
                o = _chunk_head(zs, qd, kd0, qb, kd, decay, v, glt_s, slot, h, keep, lvl)
                ggate = z_s[rows, pl.ds(o_ggate + h * gdv, gdv)]
                o = _rms(o) * gln_ref[...] * (ggate * jax.nn.sigmoid(ggate))
                o_s[rows, pl.ds(hgw + h * gdv, gdv)] = o.astype(bf16)

    a = jnp.dot(o_s[...], wout_ref[...], preferred_element_type=f32)
    for j in range(nch):
        rows = pl.ds(j * CHUNK, CHUNK)
        m = mod_ref[j if chunk_seq else 0]
        x1 = x_ref[rows, :] + m[2:3, :] * a[j * CHUNK:(j + 1) * CHUNK, :]
        x1_ref[rows, :] = x1
        hf = _rms(x1) * n2_ref[...] * (1.0 + m[4:5, :]) + m[3:4, :]
        hf_s[rows, :] = hf
        _store_rows(slab_s, j * CHUNK, hf)
    hp_ref[...] = slab_s[...].astype(bf16)
    h_hi, h_lo = _split(hf_s[...])
    nt = functools.partial(lax.dot_general, dimension_numbers=NT, preferred_element_type=f32)
    lgt_ref[...] = nt(wrt_ref[0], h_hi) + (nt(wrt_ref[1], h_hi) + nt(wrt_ref[0], h_lo))

    def store_state():
        for s in range(nslot):
            for h in range(HG_HEADS):
                hg_ref[s, h] = hgt_s[s, h].T
            for h in range(GLA_HEADS):
                gl_ref[s, h] = glt_s[s, h, :, gla_lanes(h)].T

    if chunk_seq:
        store_state()
    else:
        pl.when(i % tiles_per_seq == tiles_per_seq - 1)(store_state)


def _mixer(x, mod, hg0, gl0, p, chunk_seq, t_total, tile_offset, prev=()):
    nseq, L, d = x.shape
    t = nseq * L
    tm = MIX_TILE
    nch = tm // CHUNK
    if chunk_seq:
        assert L == CHUNK and nseq % nch == 0
        spt, tiles_per_seq = nch, 1
        seq_of = lambda i: i
    else:
        assert L % tm == 0
        spt, tiles_per_seq = 1, L // tm
        seq_of = lambda i: i // tiles_per_seq
    ne = p["w_rt"].shape[1]
    nz = p["w_in"].shape[1]
    _, hh, hdk, hdv = hg0.shape
    _, gh, gdk, gdv = gl0.shape
    const = lambda shape: pl.BlockSpec(shape, lambda i: (0,) * len(shape))
    nprev = len(prev)
    n_in = 14
    body = functools.partial(_mixer_body, nch=nch, chunk_seq=chunk_seq, tiles_per_seq=tiles_per_seq,
                             nprev=nprev)
    return pl.pallas_call(
        body,
        grid=(t // tm,),
        in_specs=[pl.BlockSpec((tm, d), lambda i: (i, 0)),
                  pl.BlockSpec((spt, 6, d), lambda i: (seq_of(i), 0, 0)),
                  pl.BlockSpec((spt, hh, hdk, hdv), lambda i: (seq_of(i), 0, 0, 0)),
                  pl.BlockSpec((spt, gh, gdk, gdv), lambda i: (seq_of(i), 0, 0, 0)),
                  const((1, d)), const((1, d)), const((d, nz)), const(p["lb_logits"].shape),
                  const(p["gk_up"].shape), const(p["gk_bias"].shape), const(p["hg_norm"].shape),
                  const(p["gla_norm"].shape), const((d, d)), const((2, ne, d))]
                 + [pl.BlockSpec(memory_space=pl.ANY)] * nprev,
        out_specs=[pl.BlockSpec((tm, d), lambda i: (i + tile_offset, 0)),
                   pl.BlockSpec((tm * NSLAB, LANE), lambda i: (i + tile_offset, 0)),
                   pl.BlockSpec((ne, tm), lambda i: (0, i + tile_offset)),
                   pl.BlockSpec((spt, hh, hdk, hdv), lambda i: (seq_of(i), 0, 0, 0)),
                   pl.BlockSpec((spt, gh, gdk, gdv), lambda i: (seq_of(i), 0, 0, 0))],
        out_shape=[jax.ShapeDtypeStruct((t_total, d), f32),
                   jax.ShapeDtypeStruct((t_total * NSLAB, LANE), bf16),
                   jax.ShapeDtypeStruct((ne, t_total), f32),
                   jax.ShapeDtypeStruct(hg0.shape, f32),
                   jax.ShapeDtypeStruct(gl0.shape, f32)],
        input_output_aliases={n_in + j: j for j in range(nprev)},
        scratch_shapes=[pltpu.VMEM((tm, d), bf16),
                        pltpu.VMEM((tm, nz), f32),
                        pltpu.VMEM((tm, d), bf16),
                        pltpu.VMEM((tm, d), f32),
                        pltpu.VMEM((tm * NSLAB, LANE), f32),
                        pltpu.VMEM((spt, hh, hdv, hdk), f32),
                        pltpu.VMEM((spt, gh, gdv, LANE), f32)],
        compiler_params=_cparams("arbitrary"),
        name="mixer_sample" if chunk_seq else "mixer_prompt",
    )(x.reshape(t, d), mod, hg0, gl0, p["norm1"], p["norm2"], p["w_in"], p["lb_logits"], p["gk_up"],
      p["gk_bias"], p["hg_norm"], p["gla_norm"], p["w_out"], p["w_rt"], *prev)


def _route_body(lg_ref, bias_ref, eidx_ref, wsel_ref, rank_ref, cnt_ref, run_s, s_s, cand_s):
    i = pl.program_id(0)
    ne, tt = lg_ref.shape
    gsz = ne // N_GROUPS
    ninf = -jnp.inf

    @pl.when(i == 0)
    def _():
        run_s[...] = jnp.zeros_like(run_s)

    row = lax.broadcasted_iota(i32, (ne, tt), 0)
    rg = lax.broadcasted_iota(i32, (gsz, tt), 0)

    gs = []
    for g in range(N_GROUPS):
        grows = pl.ds(g * gsz, gsz)
        sg = jax.nn.sigmoid(lg_ref[grows, :])
        s_s[grows, :] = sg
        blk = sg + bias_ref[grows, :]
        m1 = jnp.max(blk, axis=0, keepdims=True)
        first = jnp.min(jnp.where(blk == m1, rg, gsz), axis=0, keepdims=True)
        m2 = jnp.max(jnp.where(rg == first, ninf, blk), axis=0, keepdims=True)
        gs.append(m1 + m2)

    picked = [jnp.zeros((1, tt), f32) for _ in range(N_GROUPS)]
    for _ in range(TOPK_GROUPS):
        cur = [jnp.where(picked[g] > 0.5, ninf, gs[g]) for g in range(N_GROUPS)]
        m = functools.reduce(jnp.maximum, cur)
        gi = jnp.full((1, tt), N_GROUPS, i32)
        for g in reversed(range(N_GROUPS)):
            gi = jnp.where(cur[g] == m, g, gi)
        picked = [jnp.where(gi == g, 1.0, picked[g]) for g in range(N_GROUPS)]

    for g in range(N_GROUPS):
        grows = pl.ds(g * gsz, gsz)
        allowed = jnp.broadcast_to(picked[g], (gsz, tt)) > 0.5
        cand_s[grows, :] = jnp.where(allowed, s_s[grows, :] + bias_ref[grows, :], ninf)
    cand = cand_s[...]
    s = s_s[...]

    eis, ws = [], []
    chosen = jnp.zeros((ne, tt), jnp.bool_)
    for _ in range(TOP_K):
        m = jnp.max(cand, axis=0, keepdims=True)
        ei = jnp.min(jnp.where(cand == m, row, ne), axis=0, keepdims=True)
        hit = row == ei
        ws.append(jnp.sum(jnp.where(hit, s, 0.0), axis=0, keepdims=True))
        cand = jnp.where(hit, ninf, cand)
        chosen = chosen | hit
        eis.append(ei)
    wsum = functools.reduce(jnp.add, ws)
    scale = ROUTE_SCALE / wsum

    onehot = chosen.astype(bf16)
    ti = lax.broadcasted_iota(i32, (tt, tt), 0)
    tj = lax.broadcasted_iota(i32, (tt, tt), 1)
    before = (ti < tj).astype(bf16)
    prior = jnp.dot(onehot, before, preferred_element_type=f32) + run_s[:, 0:1]
    ct = eidx_ref.shape[2]
    for k in range(TOP_K):
        hit = row == eis[k]
        rank = jnp.sum(jnp.where(hit, prior, 0.0), axis=0, keepdims=True).astype(i32)
        wsel_ref[k:k + 1, :] = ws[k] * scale
        for part in range(tt // ct):
            eidx_ref[part, k:k + 1, :] = eis[k][:, part * ct:(part + 1) * ct]
            rank_ref[part, k:k + 1, :] = rank[:, part * ct:(part + 1) * ct]
    run_s[...] = run_s[...] + jnp.sum(chosen.astype(f32), axis=1, keepdims=True)
    cnt_ref[...] = run_s[...].astype(i32)


def _route(lgt, b_router):
    ne, t = lgt.shape
    tt, ct = ROUTE_TILE, COMB_TILE
    tiled = pl.BlockSpec((tt // ct, TOP_K, ct), lambda i: (i, 0, 0))
    tiled_shape = jax.ShapeDtypeStruct((t // ct, TOP_K, ct), i32)
    return pl.pallas_call(
        _route_body,
        grid=(t // tt,),
        in_specs=[pl.BlockSpec((ne, tt), lambda i: (0, i)),
                  pl.BlockSpec((ne, 1), lambda i: (0, 0))],
        out_specs=[tiled,
                   pl.BlockSpec((TOP_K, tt), lambda i: (0, i)),
                   tiled,
                   pl.BlockSpec((ne, LANE), lambda i: (0, 0))],
        out_shape=[tiled_shape, jax.ShapeDtypeStruct((TOP_K, t), f32), tiled_shape,
                   jax.ShapeDtypeStruct((ne, LANE), i32)],
        scratch_shapes=[pltpu.VMEM((ne, LANE), f32), pltpu.VMEM((ne, tt), f32), pltpu.VMEM((ne, tt), f32)],
        compiler_params=_cparams("arbitrary"),
        name="route",
    )(lgt, b_router.reshape(ne, 1))


def _dest_body(eidx_ref, rank_ref, pstart_ref, dest_ref):
    ne = pstart_ref.shape[0]
    ct = eidx_ref.shape[2]
    row = lax.broadcasted_iota(i32, (ne, ct), 0)
    ps = pstart_ref[...]
    for k in range(TOP_K):
        hit = row == eidx_ref[0, k:k + 1, :]
        base = jnp.sum(jnp.where(hit, ps, 0.0), axis=0, keepdims=True)
        dest_ref[0, k:k + 1, :] = base.astype(i32) + rank_ref[0, k:k + 1, :]


def _dest(eidx3, rank3, pstarts):
    ntile, _, ct = eidx3.shape
    ne = pstarts.shape[0]
    tile = pl.BlockSpec((1, TOP_K, ct), lambda i: (i, 0, 0))
    return pl.pallas_call(
        _dest_body,
        grid=(ntile,),
        in_specs=[tile, tile, pl.BlockSpec((ne, 1), lambda i: (0, 0))],
        out_specs=tile,
        out_shape=jax.ShapeDtypeStruct(eidx3.shape, i32),
        compiler_params=_cparams("arbitrary"),
        name="dest",
    )(eidx3, rank3, pstarts.astype(f32).reshape(ne, 1))


def _dispatch_body(dest_ref, padstart_ref, hf_ref, xs_ref, zero_s, sem, zsem):
    i = pl.program_id(0)
    tt = hf_ref.shape[0] // NSLAB
    ct = dest_ref.shape[2]
    ne = padstart_ref.shape[0]
    nfill = zero_s.shape[0] // NSLAB

    @pl.when(i == 0)
    def _():
        zero_s[...] = jnp.zeros_like(zero_s)

        def fill(e, c):
            pltpu.make_async_copy(zero_s, _slab(xs_ref, padstart_ref[e], nfill), zsem).start()
            return c

        lax.fori_loop(0, ne, fill, 0)

        def drain(e, c):
            pltpu.make_async_copy(zero_s, _slab(xs_ref, 0, nfill), zsem).wait()
            return c

        lax.fori_loop(0, ne, drain, 0)

    for part in range(tt // ct):
        def send(t, c, part=part):
            for k in range(TOP_K):
                pltpu.make_async_copy(_slab(hf_ref, part * ct + t, 1),
                                      _slab(xs_ref, dest_ref[part, k, t], 1), sem).start()
            return c

        lax.fori_loop(0, ct, send, 0)
    for k in range(TOP_K):
        pltpu.make_async_copy(hf_ref, _slab(xs_ref, 0, tt), sem).wait()


def _dispatch(dest3, padstart, hfs, p_rows):
    t = hfs.shape[0] // NSLAB
    tt = ROUTE_TILE
    ct = dest3.shape[2]
    return pl.pallas_call(
        _dispatch_body,
        grid=(t // tt,),
        in_specs=[pl.BlockSpec((tt // ct, TOP_K, ct), lambda i: (i, 0, 0), memory_space=pltpu.SMEM),
                  pl.BlockSpec(memory_space=pltpu.SMEM),
                  pl.BlockSpec((tt * NSLAB, LANE), lambda i: (i, 0))],
        out_specs=pl.BlockSpec(memory_space=pl.ANY),
        out_shape=jax.ShapeDtypeStruct(((p_rows + MOE_BLOCK) * NSLAB, LANE), hfs.dtype),
        scratch_shapes=[pltpu.VMEM((MOE_BLOCK * NSLAB, LANE), hfs.dtype), pltpu.SemaphoreType.DMA,
                        pltpu.SemaphoreType.DMA],
        compiler_params=_cparams("arbitrary"),
        name="dispatch",
    )(dest3, padstart, hfs)


class _Parts:
    def __init__(self, parts):
        self.parts = parts

    def start(self):
        for c in self.parts:
            c.start()

    def wait(self):
        for c in self.parts:
            c.wait()


def _experts_body(bstart_ref, nblk_ref, nused_ref, xs_ref, wg_ref, wu_ref, wd_ref, eo_ref,
                  xbuf, obuf, stage_s, wgu_s, wd_s, isem, osem):
    e = pl.program_id(0)
    blk = stage_s.shape[0] // NSLAB
    f = wg_ref.shape[2]
    nused = nused_ref[0]

    sub = blk // DMA_PARTS

    def in_copy(g):
        slot = lax.rem(g, IN_SLOTS)
        return _Parts([pltpu.make_async_copy(_slab(xs_ref, g * blk + j * sub, sub),
                                             _slab(xbuf, slot * blk + j * sub, sub), isem.at[slot])
                       for j in range(DMA_PARTS)])

    def out_copy(g):
        slot = lax.rem(g, OUT_SLOTS)
        return _Parts([pltpu.make_async_copy(_slab(obuf, slot * blk + j * sub, sub),
                                             _slab(eo_ref, g * blk + j * sub, sub), osem.at[slot])
                       for j in range(DMA_PARTS)])

    @pl.when(e == 0)
    def _():
        for g in range(IN_SLOTS - 1):
            pl.when(g < nused)(lambda g=g: in_copy(jnp.int32(g)).start())

    @pl.when(nblk_ref[e] > 0)
    def _():
        wgu_s[:, 0:f] = wg_ref[0].astype(bf16)
        wgu_s[:, f:2 * f] = wu_ref[0].astype(bf16)
        wd_s[...] = wd_ref[0].astype(bf16)

        def block(g, c):
            in_copy(g).wait()

            @pl.when(g + IN_SLOTS - 1 < nused)
            def _():
                in_copy(g + IN_SLOTS - 1).start()

            islot = pl.multiple_of(lax.rem(g, IN_SLOTS) * (blk * NSLAB), blk * NSLAB)
            stage_s[...] = xbuf[pl.ds(islot, blk * NSLAB), :].astype(f32)
            x = _load_rows(stage_s, 0, blk).astype(bf16)
            gu = jnp.dot(x, wgu_s[...], preferred_element_type=f32)
            gate = gu[:, 0:f]
            h = (gate * jax.nn.sigmoid(gate) * gu[:, f:2 * f]).astype(bf16)
            o = jnp.dot(h, wd_s[...], preferred_element_type=f32)

            @pl.when(g >= OUT_SLOTS)
            def _():
                out_copy(g - OUT_SLOTS).wait()

            _store_rows(obuf, lax.rem(g, OUT_SLOTS) * blk, o)
            out_copy(g).start()
            return c

        lax.fori_loop(bstart_ref[e], bstart_ref[e] + nblk_ref[e], block, 0)

    @pl.when(e == pl.num_programs(0) - 1)
    def _():
        for back in range(1, OUT_SLOTS + 1):
            pl.when(nused >= back)(lambda back=back: out_copy(nused - back).wait())


def _experts(bstart, nblk, nused, xs, w_gate, w_up, w_down, nb):
    ne, d, f = w_gate.shape
    assert d == NSLAB * LANE
    blk = MOE_BLOCK
    w_map = lambda e, *_: (e, 0, 0)
    grid_spec = pltpu.PrefetchScalarGridSpec(
        num_scalar_prefetch=3,
        grid=(ne,),
        in_specs=[pl.BlockSpec(memory_space=pl.ANY),
                  pl.BlockSpec((1, d, f), w_map),
                  pl.BlockSpec((1, d, f), w_map),
                  pl.BlockSpec((1, f, d), w_map)],
        out_specs=pl.BlockSpec(memory_space=pl.ANY),
        scratch_shapes=[pltpu.VMEM((IN_SLOTS * blk * NSLAB, LANE), xs.dtype),
                        pltpu.VMEM((OUT_SLOTS * blk * NSLAB, LANE), f32),
                        pltpu.VMEM((blk * NSLAB, LANE), f32),
                        pltpu.VMEM((d, 2 * f), bf16), pltpu.VMEM((f, d), bf16),
                        pltpu.SemaphoreType.DMA((IN_SLOTS,)), pltpu.SemaphoreType.DMA((OUT_SLOTS,))],
    )
    return pl.pallas_call(
        _experts_body,
        grid_spec=grid_spec,
        out_shape=jax.ShapeDtypeStruct((nb * blk * NSLAB, LANE), f32),
        compiler_params=_cparams("arbitrary"),
        name="experts",
    )(bstart, nblk, nused, xs, w_gate, w_up, w_down)


def _combine_body(dest_ref, destn_ref, eo_ref, x1_ref, hp_ref, w_ref, mod_ref, wsg_ref, wsu_ref, wsd_ref,
                  nf_ref, y_ref, buf, stage_s, sem, *, chunk_seq):
    i = pl.program_id(0)
    tt, d = x1_ref.shape
    slot = i & 1

    def gather(idx_ref, into):
        def fetch(t, c):
            for k in range(TOP_K):
                pltpu.make_async_copy(_slab(eo_ref, idx_ref[0, k, t], 1),
                                      _slab(buf, (into * TOP_K + k) * tt + t, 1), sem.at[into]).start()
            return c

        lax.fori_loop(0, tt, fetch, 0)

    pl.when(i == 0)(lambda: gather(dest_ref, slot))
    pl.when(i + 1 < pl.num_programs(0))(lambda: gather(destn_ref, 1 - slot))

    stage_s[...] = hp_ref[...].astype(f32)
    hf = _load_rows(stage_s, 0, tt).astype(bf16)
    g = jnp.dot(hf, wsg_ref[...], preferred_element_type=f32)
    u = jnp.dot(hf, wsu_ref[...], preferred_element_type=f32)
    acc = jnp.dot((g * jax.nn.sigmoid(g) * u).astype(bf16), wsd_ref[...], preferred_element_type=f32)

    for k in range(TOP_K):
        pltpu.make_async_copy(_slab(eo_ref, 0, tt), _slab(buf, (slot * TOP_K + k) * tt, tt), sem.at[slot]).wait()
    for k in range(TOP_K):
        acc = acc + _load_rows(buf, (slot * TOP_K + k) * tt, tt) * w_ref[:, k:k + 1]

    for j in range(tt // CHUNK):
        rows = pl.ds(j * CHUNK, CHUNK)
        g2 = mod_ref[j if chunk_seq else 0][5:6, :]
        y = x1_ref[rows, :] + g2 * acc[j * CHUNK:(j + 1) * CHUNK, :]
        y_ref[rows, :] = _rms(y) * nf_ref[...]


def _combine(dest3, eo, x1, hp, wsel, mod, p, nseq, seq_len, tile_offset):
    d = x1.shape[1]
    tt = COMB_TILE
    f = p["ws_gate"].shape[1]
    t = nseq * seq_len
    chunk_seq = seq_len == CHUNK
    if chunk_seq:
        spt, tiles_per_seq = tt // CHUNK, 1
        seq_of = lambda i: i
    else:
        assert seq_len % tt == 0
        spt, tiles_per_seq = 1, seq_len // tt
        seq_of = lambda i: i // tiles_per_seq
    body = functools.partial(_combine_body, chunk_seq=chunk_seq)
    const = lambda shape: pl.BlockSpec(shape, lambda i: (0,) * len(shape))
    tok = lambda w: pl.BlockSpec((tt, w), lambda i: (i + tile_offset, 0))
    ntiles = t // tt
    cur = pl.BlockSpec((1, TOP_K, tt), lambda i: (i + tile_offset, 0, 0), memory_space=pltpu.SMEM)
    nxt = pl.BlockSpec((1, TOP_K, tt), lambda i: (jnp.minimum(i + 1, ntiles - 1) + tile_offset, 0, 0),
                       memory_space=pltpu.SMEM)
    return pl.pallas_call(
        body,
        grid=(ntiles,),
        in_specs=[cur, nxt,
                  pl.BlockSpec(memory_space=pl.ANY),
                  tok(d), pl.BlockSpec((tt * NSLAB, LANE), lambda i: (i + tile_offset, 0)), tok(TOP_K),
                  pl.BlockSpec((spt, 6, d), lambda i: (seq_of(i), 0, 0)),
                  const((d, f)), const((d, f)), const((f, d)), const((1, d))],
        out_specs=pl.BlockSpec((tt, d), lambda i: (i, 0)),
        out_shape=jax.ShapeDtypeStruct((t, d), f32),
        scratch_shapes=[pltpu.VMEM((2 * TOP_K * tt * NSLAB, LANE), eo.dtype),
                        pltpu.VMEM((tt * NSLAB, LANE), f32), pltpu.SemaphoreType.DMA((2,))],
        compiler_params=_cparams("arbitrary"),
        name="combine_sample" if chunk_seq else "combine_prompt",
    )(dest3, dest3, eo, x1, hp, wsel, mod, p["ws_gate"], p["ws_up"], p["ws_down"], p["norm_final"])


def kernel(x_prompt, x_sample, state_hgrn, state_gla, c_prompt, c_sample, w_ada, b_ada, norm1, norm2, w_in,
           hg_lb_logits, gla_gk_up, gla_gk_bias, hg_out_norm, gla_out_norm, w_out, w_router, b_router,
           w_gate, w_up, w_down, ws_gate, ws_up, ws_down, norm_final):
    nbp, lp, d = x_prompt.shape
    nbs, ls, _ = x_sample.shape
    depth = w_in.shape[0]
    assert depth == 1
    ne = w_router.shape[2]
    hgw = hg_out_norm.shape[1]
    gkw = gla_gk_up.shape[2]
    rank = gla_gk_up.shape[1]
    d_in = w_in.shape[2]

    nz = d_in - rank + LANE
    w_in_p = jnp.pad(w_in[0].astype(bf16), ((0, 0), (0, nz - d_in)))
    p = dict(
        norm1=norm1[0].reshape(1, d), norm2=norm2[0].reshape(1, d), w_in=w_in_p,
        lb_logits=hg_lb_logits,
        gk_up=jnp.stack(_split(jnp.pad(gla_gk_up[0], ((0, LANE - rank), (0, 0))))),
        gk_bias=gla_gk_bias[0].reshape(1, gkw),
        hg_norm=hg_out_norm[0].reshape(1, hgw), gla_norm=gla_out_norm[0].reshape(1, -1),
        w_out=w_out[0].astype(bf16), w_rt=jnp.stack(_split(w_router[0].T)),
        ws_gate=ws_gate[0].astype(bf16), ws_up=ws_up[0].astype(bf16), ws_down=ws_down[0].astype(bf16),
        norm_final=norm_final.reshape(1, d),
    )

    c_all = jnp.concatenate([c_prompt, c_sample], axis=0)
    mod = _ada(c_all, w_ada[0], b_ada[0]).reshape(nbp + nbs, 6, d)

    zero_hg = jnp.zeros((nbp,) + state_hgrn.shape[2:], f32)
    zero_gl = jnp.zeros((nbp,) + state_gla.shape[2:], f32)
    tp, ts = nbp * lp, nbs * ls
    t = tp + ts
    x1, hp, lgt, hg_p, gl_p = _mixer(x_prompt, mod[:nbp], zero_hg, zero_gl, p, False, t, 0)
    x1, hp, lgt, hg_s, gl_s = _mixer(x_sample, mod[nbp:], state_hgrn[0], state_gla[0], p, True, t,
                                     tp // MIX_TILE, prev=(x1, hp, lgt))

    eidx3, wsel_t, rank3, cnt = _route(lgt, b_router[0])

    counts = cnt[:, 0]
    padded = (counts + MOE_BLOCK - 1) // MOE_BLOCK * MOE_BLOCK
    pends = jnp.cumsum(padded)
    pstarts = pends - padded
    nb = -(-(t * TOP_K + ne * (MOE_BLOCK - 1)) // MOE_BLOCK)
    nused = (pends[-1] // MOE_BLOCK).astype(i32).reshape(1)

    pstarts = pstarts.astype(i32)
    dest3 = _dest(eidx3, rank3, pstarts)
    xs = _dispatch(dest3, (pstarts + counts).astype(i32), hp, nb * MOE_BLOCK)
    eo = _experts(pstarts // MOE_BLOCK, (padded // MOE_BLOCK).astype(i32), nused, xs,
                  w_gate[0], w_up[0], w_down[0], nb)
    wsel = wsel_t.T
    y_prompt = _combine(dest3, eo, x1, hp, wsel, mod[:nbp], p, nbp, lp, 0)
    y_sample = _combine(dest3, eo, x1, hp, wsel, mod[nbp:], p, nbs, ls, tp // COMB_TILE)
    return (y_prompt.reshape(nbp, lp, d), y_sample.reshape(nbs, ls, d),
            hg_p[None], gl_p[None], hg_s[None], gl_s[None])
```

```python
import functools

import jax
import jax.numpy as jnp
from jax import lax
from jax.experimental import pallas as pl
from jax.experimental.pallas import tpu as pltpu

f32 = jnp.float32
bf16 = jnp.bfloat16
i32 = jnp.int32
HIGHEST = lax.Precision.HIGHEST

EPS = 1e-6
CHUNK = 64
HG_HEADS = 4
GLA_HEADS = 4
GLA_GATE_NORMALIZER = 16.0
TOP_K = 8
N_GROUPS = 8
TOPK_GROUPS = 4
ROUTE_SCALE = 2.5

LANE = 128
SUBLANE = 8
V7X_VMEM_BYTES = 64 * 1024 * 1024
VMEM_LIMIT = V7X_VMEM_BYTES - 8 * 1024 * 1024

MIX_TILE = 512
ROUTE_TILE = 512
COMB_TILE = 256
MOE_BLOCK = 512
IN_SLOTS = 3
DMA_PARTS = 1
OUT_SLOTS = 2


def _cparams(*sem):
    return pltpu.CompilerParams(dimension_semantics=sem, vmem_limit_bytes=VMEM_LIMIT)


def _sigmoid_pair(x):
    e = jnp.exp(-jnp.abs(x))
    r = 1.0 / (1.0 + e)
    er = e * r
    pos = x >= 0
    return jnp.where(pos, r, er), jnp.where(pos, er, r)


def _rms(x):
    return x * lax.rsqrt(jnp.mean(x * x, axis=-1, keepdims=True) + EPS)


def _split(x):
    hi = x.astype(bf16)
    return hi, (x - hi.astype(f32)).astype(bf16)


def _dot_split(a, b_hi, b_lo, dims):
    a_hi, a_lo = _split(a)
    dg = functools.partial(lax.dot_general, dimension_numbers=dims, preferred_element_type=f32)
    return dg(a_hi, b_hi) + (dg(a_lo, b_hi) + dg(a_hi, b_lo))


NSLAB = SUBLANE


def _slab(ref, row0, n):
    return ref.at[pl.ds(pl.multiple_of(row0 * NSLAB, NSLAB), n * NSLAB), :]


def _load_rows(ref, row0, n):
    return jnp.concatenate([ref[pl.ds(row0 * NSLAB + s, n, stride=NSLAB), :] for s in range(NSLAB)], axis=1)


def _store_rows(ref, row0, x):
    for s in range(NSLAB):
        ref[pl.ds(row0 * NSLAB + s, x.shape[0], stride=NSLAB), :] = x[:, s * LANE:(s + 1) * LANE]


def _ada_body(c_ref, w_ref, b_ref, o_ref):
    c = c_ref[...]
    a = c * jax.nn.sigmoid(c)
    o_ref[...] = jnp.dot(a, w_ref[...], preferred_element_type=f32, precision=HIGHEST) + b_ref[...]


def _ada(c_all, w, b):
    ns, d = c_all.shape
    n = w.shape[1]
    tn = 1024
    return pl.pallas_call(
        _ada_body,
        grid=(n // tn,),
        in_specs=[pl.BlockSpec((ns, d), lambda j: (0, 0)),
                  pl.BlockSpec((d, tn), lambda j: (0, j)),
                  pl.BlockSpec((1, tn), lambda j: (0, j))],
        out_specs=pl.BlockSpec((ns, tn), lambda j: (0, j)),
        out_shape=jax.ShapeDtypeStruct((ns, n), f32),
        compiler_params=_cparams("arbitrary"),
        name="ada",
    )(c_all, w, b.reshape(1, n))


LEVELS = tuple(1 << n for n in range(CHUNK.bit_length() - 1))
NT = (((1,), (1,)), ((), ()))
TN = (((0,), (0,)), ((), ()))


def _pair_masks(c):
    t = lax.broadcasted_iota(i32, (c, c), 0)
    s = lax.broadcasted_iota(i32, (c, c), 1)
    x = t ^ s
    masks = [(x == 0).astype(f32)]
    for m in LEVELS:
        masks.append(((t > s) & (x >= m) & (x < 2 * m)).astype(f32))
    return masks


def _chunk_scan(q, k, g):
    c = q.shape[0]
    row = lax.broadcasted_iota(i32, q.shape, 0)
    pre = g
    tot = g
    zs = []
    for m in LEVELS:
        upper = (row & (2 * m - 1)) >= m
        e = jnp.exp(jnp.where(upper, pre, tot - pre))
        zs.append((jnp.where(upper, q, k) * e).astype(bf16))
        below = pltpu.roll(tot, m, 0)
        above = pltpu.roll(tot, c - m, 0)
        pre = pre + jnp.where(upper, below, 0.0)
        tot = tot + jnp.where(upper, below, above)
    qb = (q * jnp.exp(pre)).astype(bf16)
    kd = (k * jnp.exp(tot - pre)).astype(bf16)
    return zs, qb, kd, tot[0:1, :]


def _chunk_head(zs, qd, kd0, qb, kd, decay, vt, st_ref, slot, head, keep, lvl):
    own = (lambda a: a) if keep is None else (lambda a: a * keep)
    att = lax.dot_general(own(qd), kd0, NT, preferred_element_type=f32) * lvl[0]
    for z, mask in zip(zs, lvl[1:]):
        att = att + lax.dot_general(own(z), z, NT, preferred_element_type=f32) * mask
    st = st_ref[slot, head]
    lhs = jnp.concatenate([own(qb), att.astype(bf16)], axis=1)
    rhs_t = jnp.concatenate([st.astype(bf16), vt], axis=1)
    o = lax.dot_general(lhs, rhs_t, NT, preferred_element_type=f32)
    st_ref[slot, head] = st * decay + jnp.dot(vt, own(kd), preferred_element_type=f32)
    return o


def _mixer_body(*refs, nch, chunk_seq, tiles_per_seq, nprev):
    (x_ref, mod_ref, hg0_ref, gl0_ref, n1_ref, n2_ref, win_ref, lbl_ref, gup_ref, gbias_ref,
     hgn_ref, gln_ref, wout_ref, wrt_ref) = refs[:14]
    (x1_ref, hp_ref, lgt_ref, hg_ref, gl_ref, h_s, z_s, o_s, hf_s, slab_s, hgt_s, glt_s) = refs[14 + nprev:]
    i = pl.program_id(0)
    d = x_ref.shape[1]
    hdk, hdv = hg0_ref.shape[2], hg0_ref.shape[3]
    gdk, gdv = gl0_ref.shape[2], gl0_ref.shape[3]
    assert hdk == LANE and LANE % gdk == 0
    hgw = HG_HEADS * hdk
    gkw = GLA_HEADS * gdk
    gvw = GLA_HEADS * gdv
    nslot = hgt_s.shape[0]
    gla_lanes = lambda h: pl.ds((h * gdk) % LANE, gdk)

    def load_state():
        for s in range(nslot):
            for h in range(HG_HEADS):
                hgt_s[s, h] = hg0_ref[s, h].T
            for h in range(GLA_HEADS):
                glt_s[s, h] = jnp.zeros(glt_s.shape[2:], f32)
                glt_s[s, h, :, gla_lanes(h)] = gl0_ref[s, h].T

    if chunk_seq:
        load_state()
    else:
        pl.when(i % tiles_per_seq == 0)(load_state)

    lbl = lbl_ref[...]
    lbe = jnp.exp(lbl - jnp.max(lbl, axis=0, keepdims=True))
    lb = lbe[0:1, :] / jnp.sum(lbe, axis=0, keepdims=True)

    for j in range(nch):
        rows = pl.ds(j * CHUNK, CHUNK)
        m = mod_ref[j if chunk_seq else 0]
        xn = _rms(x_ref[rows, :]) * n1_ref[...]
        h_s[rows, :] = (xn * (1.0 + m[1:2, :]) + m[0:1, :]).astype(bf16)

    nz = z_s.shape[1]
    step = 512
    for n0 in range(0, nz, step):
        n1 = min(n0 + step, nz)
        z_s[:, n0:n1] = jnp.dot(h_s[...], win_ref[:, n0:n1], preferred_element_type=f32)

    o_hq, o_hf, o_hi, o_hgate = 0, hgw, 2 * hgw, 3 * hgw
    o_gq = 4 * hgw
    o_gk = o_gq + gkw
    o_gv = o_gk + gkw
    o_ggate = o_gv + gvw
    o_glr = o_ggate + gvw

    lvl = _pair_masks(CHUNK)
    lane = lax.broadcasted_iota(i32, (CHUNK, LANE), 1)
    heads_per_block = LANE // gdk
    for j in range(nch):
        rows = pl.ds(j * CHUNK, CHUNK)
        slot = j if chunk_seq else 0
        for h in range(HG_HEADS):
            lbh = lb[:, h * hdk:(h + 1) * hdk]
            hq = z_s[rows, pl.ds(o_hq + h * hdk, hdk)]
            sig, nsig = _sigmoid_pair(z_s[rows, pl.ds(o_hf + h * hdk, hdk)])
            g = jnp.log(lbh + (1.0 - lbh) * sig)
            k = (1.0 - lbh) * nsig
            q = hq * jax.nn.sigmoid(hq) * (hdk ** -0.5)
            zs, qb, kd, b_last = _chunk_scan(q, k, g)
            v = z_s[rows, pl.ds(o_hi + h * hdv, hdv)].T.astype(bf16)
            o = _chunk_head(zs, q.astype(bf16), k.astype(bf16), qb, kd, jnp.exp(b_last), v,
                            hgt_s, slot, h, None, lvl)
            hgate = z_s[rows, pl.ds(o_hgate + h * hdv, hdv)]
            o = _rms(o) * hgn_ref[:, pl.ds(h * hdv, hdv)] * jax.nn.sigmoid(hgate)
            o_s[rows, pl.ds(h * hdv, hdv)] = o.astype(bf16)
        u = _dot_split(z_s[rows, pl.ds(o_glr, LANE)], gup_ref[0], gup_ref[1], (((1,), (0,)), ((), ())))
        u = u + gbias_ref[...]
        loga = (jnp.minimum(u, 0.0) - jnp.log1p(jnp.exp(-jnp.abs(u)))) * (1.0 / GLA_GATE_NORMALIZER)
        for blk in range(GLA_HEADS // heads_per_block):
            q = z_s[rows, pl.ds(o_gq + blk * LANE, LANE)] * (gdk ** -0.5)
            k = z_s[rows, pl.ds(o_gk + blk * LANE, LANE)]
            zs, qb, kd, b_last = _chunk_scan(q, k, loga[:, blk * LANE:(blk + 1) * LANE])
            qd, kd0, decay = q.astype(bf16), k.astype(bf16), jnp.exp(b_last)
            for h in range(blk * heads_per_block, (blk + 1) * heads_per_block):
                lo = (h * gdk) % LANE
                keep = ((lane >= lo) & (lane < lo + gdk)).astype(bf16)
                v = z_s[rows, pl.ds(o_gv + h * gdv, gdv)].T.astype(bf16)
                o = _chunk_head(zs, qd, kd0, qb, kd, decay, v, glt_s, slot, h, keep, lvl)
                ggate = z_s[rows, pl.ds(o_ggate + h * gdv, gdv)]
                o = _rms(o) * gln_ref[...] * (ggate * jax.nn.sigmoid(ggate))
                o_s[rows, pl.ds(hgw + h * gdv, gdv)] = o.astype(bf16)

    a = jnp.dot(o_s[...], wout_ref[...], preferred_element_type=f32)
    for j in range(nch):
        rows = pl.ds(j * CHUNK, CHUNK)
        m = mod_ref[j if chunk_seq else 0]
        x1 = x_ref[rows, :] + m[2:3, :] * a[j * CHUNK:(j + 1) * CHUNK, :]
        x1_ref[rows, :] = x1
        hf = _rms(x1) * n2_ref[...] * (1.0 + m[4:5, :]) + m[3:4, :]
        hf_s[rows, :] = hf
        _store_rows(slab_s, j * CHUNK, hf)
    hp_ref[...] = slab_s[...].astype(bf16)
    h_hi, h_lo = _split(hf_s[...])
    nt = functools.partial(lax.dot_general, dimension_numbers=NT, preferred_element_type=f32)
    lgt_ref[...] = nt(wrt_ref[0], h_hi) + (nt(wrt_ref[1], h_hi) + nt(wrt_ref[0], h_lo))

    def store_state():
        for s in range(nslot):
            for h in range(HG_HEADS):
                hg_ref[s, h] = hgt_s[s, h].T
            for h in range(GLA_HEADS):
                gl_ref[s, h] = glt_s[s, h, :, gla_lanes(h)].T

    if chunk_seq:
        store_state()
    else:
        pl.when(i % tiles_per_seq == tiles_per_seq - 1)(store_state)


def _mixer(x, mod, hg0, gl0, p, chunk_seq, t_total, tile_offset, prev=()):
    nseq, L, d = x.shape
    t = nseq * L
    tm = MIX_TILE
    nch = tm // CHUNK
    if chunk_seq:
        assert L == CHUNK and nseq % nch == 0
        spt, tiles_per_seq = nch, 1
        seq_of = lambda i: i
    else:
        assert L % tm == 0
        spt, tiles_per_seq = 1, L // tm
        seq_of = lambda i: i // tiles_per_seq
    ne = p["w_rt"].shape[1]
    nz = p["w_in"].shape[1]
    _, hh, hdk, hdv = hg0.shape
    _, gh, gdk, gdv = gl0.shape
    const = lambda shape: pl.BlockSpec(shape, lambda i: (0,) * len(shape))
    nprev = len(prev)
    n_in = 14
    body = functools.partial(_mixer_body, nch=nch, chunk_seq=chunk_seq, tiles_per_seq=tiles_per_seq,
                             nprev=nprev)
    return pl.pallas_call(
        body,
        grid=(t // tm,),
        in_specs=[pl.BlockSpec((tm, d), lambda i: (i, 0)),
                  pl.BlockSpec((spt, 6, d), lambda i: (seq_of(i), 0, 0)),
                  pl.BlockSpec((spt, hh, hdk, hdv), lambda i: (seq_of(i), 0, 0, 0)),
                  pl.BlockSpec((spt, gh, gdk, gdv), lambda i: (seq_of(i), 0, 0, 0)),
                  const((1, d)), const((1, d)), const((d, nz)), const(p["lb_logits"].shape),
                  const(p["gk_up"].shape), const(p["gk_bias"].shape), const(p["hg_norm"].shape),
                  const(p["gla_norm"].shape), const((d, d)), const((2, ne, d))]
                 + [pl.BlockSpec(memory_space=pl.ANY)] * nprev,
        out_specs=[pl.BlockSpec((tm, d), lambda i: (i + tile_offset, 0)),
                   pl.BlockSpec((tm * NSLAB, LANE), lambda i: (i + tile_offset, 0)),
                   pl.BlockSpec((ne, tm), lambda i: (0, i + tile_offset)),
                   pl.BlockSpec((spt, hh, hdk, hdv), lambda i: (seq_of(i), 0, 0, 0)),
                   pl.BlockSpec((spt, gh, gdk, gdv), lambda i: (seq_of(i), 0, 0, 0))],
        out_shape=[jax.ShapeDtypeStruct((t_total, d), f32),
                   jax.ShapeDtypeStruct((t_total * NSLAB, LANE), bf16),
                   jax.ShapeDtypeStruct((ne, t_total), f32),
                   jax.ShapeDtypeStruct(hg0.shape, f32),
                   jax.ShapeDtypeStruct(gl0.shape, f32)],
        input_output_aliases={n_in + j: j for j in range(nprev)},
        scratch_shapes=[pltpu.VMEM((tm, d), bf16),
                        pltpu.VMEM((tm, nz), f32),
                        pltpu.VMEM((tm, d), bf16),
                        pltpu.VMEM((tm, d), f32),
                        pltpu.VMEM((tm * NSLAB, LANE), f32),
                        pltpu.VMEM((spt, hh, hdv, hdk), f32),
                        pltpu.VMEM((spt, gh, gdv, LANE), f32)],
        compiler_params=_cparams("arbitrary"),
        name="mixer_sample" if chunk_seq else "mixer_prompt",
    )(x.reshape(t, d), mod, hg0, gl0, p["norm1"], p["norm2"], p["w_in"], p["lb_logits"], p["gk_up"],
      p["gk_bias"], p["hg_norm"], p["gla_norm"], p["w_out"], p["w_rt"], *prev)


def _route_body(lg_ref, bias_ref, eidx_ref, wsel_ref, rank_ref, cnt_ref, run_s, s_s, cand_s):
    i = pl.program_id(0)
    ne, tt = lg_ref.shape
    gsz = ne // N_GROUPS
    ninf = -jnp.inf

    @pl.when(i == 0)
    def _():
        run_s[...] = jnp.zeros_like(run_s)

    row = lax.broadcasted_iota(i32, (ne, tt), 0)
    rg = lax.broadcasted_iota(i32, (gsz, tt), 0)

    gs = []
    for g in range(N_GROUPS):
        grows = pl.ds(g * gsz, gsz)
        sg = jax.nn.sigmoid(lg_ref[grows, :])
        s_s[grows, :] = sg
        blk = sg + bias_ref[grows, :]
        m1 = jnp.max(blk, axis=0, keepdims=True)
        first = jnp.min(jnp.where(blk == m1, rg, gsz), axis=0, keepdims=True)
        m2 = jnp.max(jnp.where(rg == first, ninf, blk), axis=0, keepdims=True)
        gs.append(m1 + m2)

    picked = [jnp.zeros((1, tt), f32) for _ in range(N_GROUPS)]
    for _ in range(TOPK_GROUPS):
        cur = [jnp.where(picked[g] > 0.5, ninf, gs[g]) for g in range(N_GROUPS)]
        m = functools.reduce(jnp.maximum, cur)
        gi = jnp.full((1, tt), N_GROUPS, i32)
        for g in reversed(range(N_GROUPS)):
            gi = jnp.where(cur[g] == m, g, gi)
        picked = [jnp.where(gi == g, 1.0, picked[g]) for g in range(N_GROUPS)]

    for g in range(N_GROUPS):
        grows = pl.ds(g * gsz, gsz)
        allowed = jnp.broadcast_to(picked[g], (gsz, tt)) > 0.5
        cand_s[grows, :] = jnp.where(allowed, s_s[grows, :] + bias_ref[grows, :], ninf)
    cand = cand_s[...]
    s = s_s[...]

    eis, ws = [], []
    chosen = jnp.zeros((ne, tt), jnp.bool_)
    for _ in range(TOP_K):
        m = jnp.max(cand, axis=0, keepdims=True)
        ei = jnp.min(jnp.where(cand == m, row, ne), axis=0, keepdims=True)
        hit = row == ei
        ws.append(jnp.sum(jnp.where(hit, s, 0.0), axis=0, keepdims=True))
        cand = jnp.where(hit, ninf, cand)
        chosen = chosen | hit
        eis.append(ei)
    wsum = functools.reduce(jnp.add, ws)
    scale = ROUTE_SCALE / wsum

    onehot = chosen.astype(bf16)
    ti = lax.broadcasted_iota(i32, (tt, tt), 0)
    tj = lax.broadcasted_iota(i32, (tt, tt), 1)
    before = (ti < tj).astype(bf16)
    prior = jnp.dot(onehot, before, preferred_element_type=f32) + run_s[:, 0:1]
    ct = eidx_ref.shape[2]
    for k in range(TOP_K):
        hit = row == eis[k]
        rank = jnp.sum(jnp.where(hit, prior, 0.0), axis=0, keepdims=True).astype(i32)
        wsel_ref[k:k + 1, :] = ws[k] * scale
        for part in range(tt // ct):
            eidx_ref[part, k:k + 1, :] = eis[k][:, part * ct:(part + 1) * ct]
            rank_ref[part, k:k + 1, :] = rank[:, part * ct:(part + 1) * ct]
    run_s[...] = run_s[...] + jnp.sum(chosen.astype(f32), axis=1, keepdims=True)
    cnt_ref[...] = run_s[...].astype(i32)


def _route(lgt, b_router):
    ne, t = lgt.shape
    tt, ct = ROUTE_TILE, COMB_TILE
    tiled = pl.BlockSpec((tt // ct, TOP_K, ct), lambda i: (i, 0, 0))
    tiled_shape = jax.ShapeDtypeStruct((t // ct, TOP_K, ct), i32)
    return pl.pallas_call(
        _route_body,
        grid=(t // tt,),
        in_specs=[pl.BlockSpec((ne, tt), lambda i: (0, i)),
                  pl.BlockSpec((ne, 1), lambda i: (0, 0))],
        out_specs=[tiled,
                   pl.BlockSpec((TOP_K, tt), lambda i: (0, i)),
                   tiled,
                   pl.BlockSpec((ne, LANE), lambda i: (0, 0))],
        out_shape=[tiled_shape, jax.ShapeDtypeStruct((TOP_K, t), f32), tiled_shape,
                   jax.ShapeDtypeStruct((ne, LANE), i32)],
        scratch_shapes=[pltpu.VMEM((ne, LANE), f32), pltpu.VMEM((ne, tt), f32), pltpu.VMEM((ne, tt), f32)],
        compiler_params=_cparams("arbitrary"),
        name="route",
    )(lgt, b_router.reshape(ne, 1))


def _dest_body(eidx_ref, rank_ref, pstart_ref, dest_ref):
    ne = pstart_ref.shape[0]
    ct = eidx_ref.shape[2]
    row = lax.broadcasted_iota(i32, (ne, ct), 0)
    ps = pstart_ref[...]
    for k in range(TOP_K):
        hit = row == eidx_ref[0, k:k + 1, :]
        base = jnp.sum(jnp.where(hit, ps, 0.0), axis=0, keepdims=True)
        dest_ref[0, k:k + 1, :] = base.astype(i32) + rank_ref[0, k:k + 1, :]


def _dest(eidx3, rank3, pstarts):
    ntile, _, ct = eidx3.shape
    ne = pstarts.shape[0]
    tile = pl.BlockSpec((1, TOP_K, ct), lambda i: (i, 0, 0))
    return pl.pallas_call(
        _dest_body,
        grid=(ntile,),
        in_specs=[tile, tile, pl.BlockSpec((ne, 1), lambda i: (0, 0))],
        out_specs=tile,
        out_shape=jax.ShapeDtypeStruct(eidx3.shape, i32),
        compiler_params=_cparams("arbitrary"),
        name="dest",
    )(eidx3, rank3, pstarts.astype(f32).reshape(ne, 1))


def _dispatch_body(dest_ref, padstart_ref, hf_ref, xs_ref, zero_s, sem, zsem):
    i = pl.program_id(0)
    tt = hf_ref.shape[0] // NSLAB
    ct = dest_ref.shape[2]
    ne = padstart_ref.shape[0]
    nfill = zero_s.shape[0] // NSLAB

    @pl.when(i == 0)
    def _():
        zero_s[...] = jnp.zeros_like(zero_s)

        def fill(e, c):
            pltpu.make_async_copy(zero_s, _slab(xs_ref, padstart_ref[e], nfill), zsem).start()
            return c

        lax.fori_loop(0, ne, fill, 0)

        def drain(e, c):
            pltpu.make_async_copy(zero_s, _slab(xs_ref, 0, nfill), zsem).wait()
            return c

        lax.fori_loop(0, ne, drain, 0)

    for part in range(tt // ct):
        def send(t, c, part=part):
            for k in range(TOP_K):
                pltpu.make_async_copy(_slab(hf_ref, part * ct + t, 1),
                                      _slab(xs_ref, dest_ref[part, k, t], 1), sem).start()
            return c

        lax.fori_loop(0, ct, send, 0)
    for k in range(TOP_K):
        pltpu.make_async_copy(hf_ref, _slab(xs_ref, 0, tt), sem).wait()


def _dispatch(dest3, padstart, hfs, p_rows):
    t = hfs.shape[0] // NSLAB
    tt = ROUTE_TILE
    ct = dest3.shape[2]
    return pl.pallas_call(
        _dispatch_body,
        grid=(t // tt,),
        in_specs=[pl.BlockSpec((tt // ct, TOP_K, ct), lambda i: (i, 0, 0), memory_space=pltpu.SMEM),
                  pl.BlockSpec(memory_space=pltpu.SMEM),
                  pl.BlockSpec((tt * NSLAB, LANE), lambda i: (i, 0))],
        out_specs=pl.BlockSpec(memory_space=pl.ANY),
        out_shape=jax.ShapeDtypeStruct(((p_rows + MOE_BLOCK) * NSLAB, LANE), hfs.dtype),
        scratch_shapes=[pltpu.VMEM((MOE_BLOCK * NSLAB, LANE), hfs.dtype), pltpu.SemaphoreType.DMA,
                        pltpu.SemaphoreType.DMA],
        compiler_params=_cparams("arbitrary"),
        name="dispatch",
    )(dest3, padstart, hfs)


class _Parts:
    def __init__(self, parts):
        self.parts = parts

    def start(self):
        for c in self.parts:
            c.start()

    def wait(self):
        for c in self.parts:
            c.wait()


def _experts_body(bstart_ref, nblk_ref, nused_ref, xs_ref, wg_ref, wu_ref, wd_ref, eo_ref,
                  xbuf, obuf, stage_s, wgu_s, wd_s, isem, osem):
    e = pl.program_id(0)
    blk = stage_s.shape[0] // NSLAB
    f = wg_ref.shape[2]
    nused = nused_ref[0]

    sub = blk // DMA_PARTS

    def in_copy(g):
        slot = lax.rem(g, IN_SLOTS)
        return _Parts([pltpu.make_async_copy(_slab(xs_ref, g * blk + j * sub, sub),
                                             _slab(xbuf, slot * blk + j * sub, sub), isem.at[slot])
                       for j in range(DMA_PARTS)])

    def out_copy(g):
        slot = lax.rem(g, OUT_SLOTS)
        return _Parts([pltpu.make_async_copy(_slab(obuf, slot * blk + j * sub, sub),
                                             _slab(eo_ref, g * blk + j * sub, sub), osem.at[slot])
                       for j in range(DMA_PARTS)])

    @pl.when(e == 0)
    def _():
        for g in range(IN_SLOTS - 1):
            pl.when(g < nused)(lambda g=g: in_copy(jnp.int32(g)).start())

    @pl.when(nblk_ref[e] > 0)
    def _():
        wgu_s[:, 0:f] = wg_ref[0].astype(bf16)
        wgu_s[:, f:2 * f] = wu_ref[0].astype(bf16)
        wd_s[...] = wd_ref[0].astype(bf16)

        def block(g, c):
            in_copy(g).wait()

            @pl.when(g + IN_SLOTS - 1 < nused)
            def _():
                in_copy(g + IN_SLOTS - 1).start()

            islot = pl.multiple_of(lax.rem(g, IN_SLOTS) * (blk * NSLAB), blk * NSLAB)
            stage_s[...] = xbuf[pl.ds(islot, blk * NSLAB), :].astype(f32)
            x = _load_rows(stage_s, 0, blk).astype(bf16)
            gu = jnp.dot(x, wgu_s[...], preferred_element_type=f32)
            gate = gu[:, 0:f]
            h = (gate * jax.nn.sigmoid(gate) * gu[:, f:2 * f]).astype(bf16)
            o = jnp.dot(h, wd_s[...], preferred_element_type=f32)

            @pl.when(g >= OUT_SLOTS)
            def _():
                out_copy(g - OUT_SLOTS).wait()

            _store_rows(obuf, lax.rem(g, OUT_SLOTS) * blk, o)
            out_copy(g).start()
            return c

        lax.fori_loop(bstart_ref[e], bstart_ref[e] + nblk_ref[e], block, 0)

    @pl.when(e == pl.num_programs(0) - 1)
    def _():
        for back in range(1, OUT_SLOTS + 1):
            pl.when(nused >= back)(lambda back=back: out_copy(nused - back).wait())


def _experts(bstart, nblk, nused, xs, w_gate, w_up, w_down, nb):
    ne, d, f = w_gate.shape
    assert d == NSLAB * LANE
    blk = MOE_BLOCK
    w_map = lambda e, *_: (e, 0, 0)
    grid_spec = pltpu.PrefetchScalarGridSpec(
        num_scalar_prefetch=3,
        grid=(ne,),
        in_specs=[pl.BlockSpec(memory_space=pl.ANY),
                  pl.BlockSpec((1, d, f), w_map),
                  pl.BlockSpec((1, d, f), w_map),
                  pl.BlockSpec((1, f, d), w_map)],
        out_specs=pl.BlockSpec(memory_space=pl.ANY),
        scratch_shapes=[pltpu.VMEM((IN_SLOTS * blk * NSLAB, LANE), xs.dtype),
                        pltpu.VMEM((OUT_SLOTS * blk * NSLAB, LANE), f32),
                        pltpu.VMEM((blk * NSLAB, LANE), f32),
                        pltpu.VMEM((d, 2 * f), bf16), pltpu.VMEM((f, d), bf16),
                        pltpu.SemaphoreType.DMA((IN_SLOTS,)), pltpu.SemaphoreType.DMA((OUT_SLOTS,))],
    )
    return pl.pallas_call(
        _experts_body,
        grid_spec=grid_spec,
        out_shape=jax.ShapeDtypeStruct((nb * blk * NSLAB, LANE), f32),
        compiler_params=_cparams("arbitrary"),
        name="experts",
    )(bstart, nblk, nused, xs, w_gate, w_up, w_down)


def _combine_body(dest_ref, destn_ref, eo_ref, x1_ref, hp_ref, w_ref, mod_ref, wsg_ref, wsu_ref, wsd_ref,
                  nf_ref, y_ref, buf, stage_s, sem, *, chunk_seq):
    i = pl.program_id(0)
    tt, d = x1_ref.shape
    slot = i & 1

    def gather(idx_ref, into):
        def fetch(t, c):
            for k in range(TOP_K):
                pltpu.make_async_copy(_slab(eo_ref, idx_ref[0, k, t], 1),
                                      _slab(buf, (into * TOP_K + k) * tt + t, 1), sem.at[into]).start()
            return c

        lax.fori_loop(0, tt, fetch, 0)

    pl.when(i == 0)(lambda: gather(dest_ref, slot))
    pl.when(i + 1 < pl.num_programs(0))(lambda: gather(destn_ref, 1 - slot))

    stage_s[...] = hp_ref[...].astype(f32)
    hf = _load_rows(stage_s, 0, tt).astype(bf16)
    g = jnp.dot(hf, wsg_ref[...], preferred_element_type=f32)
    u = jnp.dot(hf, wsu_ref[...], preferred_element_type=f32)
    acc = jnp.dot((g * jax.nn.sigmoid(g) * u).astype(bf16), wsd_ref[...], preferred_element_type=f32)

    for k in range(TOP_K):
        pltpu.make_async_copy(_slab(eo_ref, 0, tt), _slab(buf, (slot * TOP_K + k) * tt, tt), sem.at[slot]).wait()
    for k in range(TOP_K):
        acc = acc + _load_rows(buf, (slot * TOP_K + k) * tt, tt) * w_ref[:, k:k + 1]

    for j in range(tt // CHUNK):
        rows = pl.ds(j * CHUNK, CHUNK)
        g2 = mod_ref[j if chunk_seq else 0][5:6, :]
        y = x1_ref[rows, :] + g2 * acc[j * CHUNK:(j + 1) * CHUNK, :]
        y_ref[rows, :] = _rms(y) * nf_ref[...]


def _combine(dest3, eo, x1, hp, wsel, mod, p, nseq, seq_len, tile_offset):
    d = x1.shape[1]
    tt = COMB_TILE
    f = p["ws_gate"].shape[1]
    t = nseq * seq_len
    chunk_seq = seq_len == CHUNK
    if chunk_seq:
        spt, tiles_per_seq = tt // CHUNK, 1
        seq_of = lambda i: i
    else:
        assert seq_len % tt == 0
        spt, tiles_per_seq = 1, seq_len // tt
        seq_of = lambda i: i // tiles_per_seq
    body = functools.partial(_combine_body, chunk_seq=chunk_seq)
    const = lambda shape: pl.BlockSpec(shape, lambda i: (0,) * len(shape))
    tok = lambda w: pl.BlockSpec((tt, w), lambda i: (i + tile_offset, 0))
    ntiles = t // tt
    cur = pl.BlockSpec((1, TOP_K, tt), lambda i: (i + tile_offset, 0, 0), memory_space=pltpu.SMEM)
    nxt = pl.BlockSpec((1, TOP_K, tt), lambda i: (jnp.minimum(i + 1, ntiles - 1) + tile_offset, 0, 0),
                       memory_space=pltpu.SMEM)
    return pl.pallas_call(
        body,
        grid=(ntiles,),
        in_specs=[cur, nxt,
                  pl.BlockSpec(memory_space=pl.ANY),
                  tok(d), pl.BlockSpec((tt * NSLAB, LANE), lambda i: (i + tile_offset, 0)), tok(TOP_K),
                  pl.BlockSpec((spt, 6, d), lambda i: (seq_of(i), 0, 0)),
                  const((d, f)), const((d, f)), const((f, d)), const((1, d))],
        out_specs=pl.BlockSpec((tt, d), lambda i: (i, 0)),
        out_shape=jax.ShapeDtypeStruct((t, d), f32),
        scratch_shapes=[pltpu.VMEM((2 * TOP_K * tt * NSLAB, LANE), eo.dtype),
                        pltpu.VMEM((tt * NSLAB, LANE), f32), pltpu.SemaphoreType.DMA((2,))],
        compiler_params=_cparams("arbitrary"),
        name="combine_sample" if chunk_seq else "combine_prompt",
    )(dest3, dest3, eo, x1, hp, wsel, mod, p["ws_gate"], p["ws_up"], p["ws_down"], p["norm_final"])


def kernel(x_prompt, x_sample, state_hgrn, state_gla, c_prompt, c_sample, w_ada, b_ada, norm1, norm2, w_in,
           hg_lb_logits, gla_gk_up, gla_gk_bias, hg_out_norm, gla_out_norm, w_out, w_router, b_router,
           w_gate, w_up, w_down, ws_gate, ws_up, ws_down, norm_final):
    nbp, lp, d = x_prompt.shape
    nbs, ls, _ = x_sample.shape
    depth = w_in.shape[0]
    assert depth == 1
    ne = w_router.shape[2]
    hgw = hg_out_norm.shape[1]
    gkw = gla_gk_up.shape[2]
    rank = gla_gk_up.shape[1]
    d_in = w_in.shape[2]

    nz = d_in - rank + LANE
    w_in_p = jnp.pad(w_in[0].astype(bf16), ((0, 0), (0, nz - d_in)))
    p = dict(
        norm1=norm1[0].reshape(1, d), norm2=norm2[0].reshape(1, d), w_in=w_in_p,
        lb_logits=hg_lb_logits,
        gk_up=jnp.stack(_split(jnp.pad(gla_gk_up[0], ((0, LANE - rank), (0, 0))))),
        gk_bias=gla_gk_bias[0].reshape(1, gkw),
        hg_norm=hg_out_norm[0].reshape(1, hgw), gla_norm=gla_out_norm[0].reshape(1, -1),
        w_out=w_out[0].astype(bf16), w_rt=jnp.stack(_split(w_router[0].T)),
        ws_gate=ws_gate[0].astype(bf16), ws_up=ws_up[0].astype(bf16), ws_down=ws_down[0].astype(bf16),
        norm_final=norm_final.reshape(1, d),
    )

    c_all = jnp.concatenate([c_prompt, c_sample], axis=0)
    mod = _ada(c_all, w_ada[0], b_ada[0]).reshape(nbp + nbs, 6, d)

    zero_hg = jnp.zeros((nbp,) + state_hgrn.shape[2:], f32)
    zero_gl = jnp.zeros((nbp,) + state_gla.shape[2:], f32)
    tp, ts = nbp * lp, nbs * ls
    t = tp + ts
    x1, hp, lgt, hg_p, gl_p = _mixer(x_prompt, mod[:nbp], zero_hg, zero_gl, p, False, t, 0)
    x1, hp, lgt, hg_s, gl_s = _mixer(x_sample, mod[nbp:], state_hgrn[0], state_gla[0], p, True, t,
                                     tp // MIX_TILE, prev=(x1, hp, lgt))

    eidx3, wsel_t, rank3, cnt = _route(lgt, b_router[0])

    counts = cnt[:, 0]
    padded = (counts + MOE_BLOCK - 1) // MOE_BLOCK * MOE_BLOCK
    pends = jnp.cumsum(padded)
    pstarts = pends - padded
    nb = -(-(t * TOP_K + ne * (MOE_BLOCK - 1)) // MOE_BLOCK)
    nused = (pends[-1] // MOE_BLOCK).astype(i32).reshape(1)

    pstarts = pstarts.astype(i32)
    dest3 = _dest(eidx3, rank3, pstarts)
    xs = _dispatch(dest3, (pstarts + counts).astype(i32), hp, nb * MOE_BLOCK)
    eo = _experts(pstarts // MOE_BLOCK, (padded // MOE_BLOCK).astype(i32), nused, xs,
                  w_gate[0], w_up[0], w_down[0], nb)
    wsel = wsel_t.T
    y_prompt = _combine(dest3, eo, x1, hp, wsel, mod[:nbp], p, nbp, lp, 0)
    y_sample = _combine(dest3, eo, x1, hp, wsel, mod[nbp:], p, nbs, ls, tp // COMB_TILE)
    return (y_prompt.reshape(nbp, lp, d), y_sample.reshape(nbs, ls, d),
            hg_p[None], gl_p[None], hg_s[None], gl_s[None])
```

```python
import functools

import jax
import jax.numpy as jnp
from jax import lax
from jax.experimental import pallas as pl
from jax.experimental.pallas import tpu as pltpu

f32 = jnp.float32
bf16 = jnp.bfloat16
i32 = jnp.int32
HIGHEST = lax.Precision.HIGHEST

EPS = 1e-6
CHUNK = 64
HG_HEADS = 4
GLA_HEADS = 4
GLA_GATE_NORMALIZER = 16.0
TOP_K = 8
N_GROUPS = 8
TOPK_GROUPS = 4
ROUTE_SCALE = 2.5

LANE = 128
SUBLANE = 8
V7X_VMEM_BYTES = 64 * 1024 * 1024
VMEM_LIMIT = V7X_VMEM_BYTES - 8 * 1024 * 1024

MIX_TILE = 512
ROUTE_TILE = 512
COMB_TILE = 256
MOE_BLOCK = 1024
IN_SLOTS = 3
DMA_PARTS = 1
OUT_SLOTS = 2


def _cparams(*sem):
    return pltpu.CompilerParams(dimension_semantics=sem, vmem_limit_bytes=VMEM_LIMIT)


def _sigmoid_pair(x):
    e = jnp.exp(-jnp.abs(x))
    r = 1.0 / (1.0 + e)
    er = e * r
    pos = x >= 0
    return jnp.where(pos, r, er), jnp.where(pos, er, r)


def _rms(x):
    return x * lax.rsqrt(jnp.mean(x * x, axis=-1, keepdims=True) + EPS)


def _split(x):
    hi = x.astype(bf16)
    return hi, (x - hi.astype(f32)).astype(bf16)


def _dot_split(a, b_hi, b_lo, dims):
    a_hi, a_lo = _split(a)
    dg = functools.partial(lax.dot_general, dimension_numbers=dims, preferred_element_type=f32)
    return dg(a_hi, b_hi) + (dg(a_lo, b_hi) + dg(a_hi, b_lo))


NSLAB = SUBLANE


def _slab(ref, row0, n):
    return ref.at[pl.ds(pl.multiple_of(row0 * NSLAB, NSLAB), n * NSLAB), :]


def _load_rows(ref, row0, n):
    return jnp.concatenate([ref[pl.ds(row0 * NSLAB + s, n, stride=NSLAB), :] for s in range(NSLAB)], axis=1)


def _store_rows(ref, row0, x):
    for s in range(NSLAB):
        ref[pl.ds(row0 * NSLAB + s, x.shape[0], stride=NSLAB), :] = x[:, s * LANE:(s + 1) * LANE]


def _ada_body(c_ref, w_ref, b_ref, o_ref):
    c = c_ref[...]
    a = c * jax.nn.sigmoid(c)
    o_ref[...] = jnp.dot(a, w_ref[...], preferred_element_type=f32, precision=HIGHEST) + b_ref[...]


def _ada(c_all, w, b):
    ns, d = c_all.shape
    n = w.shape[1]
    tn = 1024
    return pl.pallas_call(
        _ada_body,
        grid=(n // tn,),
        in_specs=[pl.BlockSpec((ns, d), lambda j: (0, 0)),
                  pl.BlockSpec((d, tn), lambda j: (0, j)),
                  pl.BlockSpec((1, tn), lambda j: (0, j))],
        out_specs=pl.BlockSpec((ns, tn), lambda j: (0, j)),
        out_shape=jax.ShapeDtypeStruct((ns, n), f32),
        compiler_params=_cparams("arbitrary"),
        name="ada",
    )(c_all, w, b.reshape(1, n))


LEVELS = tuple(1 << n for n in range(CHUNK.bit_length() - 1))
NT = (((1,), (1,)), ((), ()))
TN = (((0,), (0,)), ((), ()))


def _pair_masks(c):
    t = lax.broadcasted_iota(i32, (c, c), 0)
    s = lax.broadcasted_iota(i32, (c, c), 1)
    x = t ^ s
    masks = [(x == 0).astype(f32)]
    for m in LEVELS:
        masks.append(((t > s) & (x >= m) & (x < 2 * m)).astype(f32))
    return masks


def _chunk_scan(q, k, g):
    c = q.shape[0]
    row = lax.broadcasted_iota(i32, q.shape, 0)
    pre = g
    tot = g
    zs = []
    for m in LEVELS:
        upper = (row & (2 * m - 1)) >= m
        e = jnp.exp(jnp.where(upper, pre, tot - pre))
        zs.append((jnp.where(upper, q, k) * e).astype(bf16))
        below = pltpu.roll(tot, m, 0)
        above = pltpu.roll(tot, c - m, 0)
        pre = pre + jnp.where(upper, below, 0.0)
        tot = tot + jnp.where(upper, below, above)
    qb = (q * jnp.exp(pre)).astype(bf16)
    kd = (k * jnp.exp(tot - pre)).astype(bf16)
    return zs, qb, kd, tot[0:1, :]


def _chunk_head(zs, qd, kd0, qb, kd, decay, vt, st_ref, slot, head, keep, lvl):
    own = (lambda a: a) if keep is None else (lambda a: a * keep)
    att = lax.dot_general(own(qd), kd0, NT, preferred_element_type=f32) * lvl[0]
    for z, mask in zip(zs, lvl[1:]):
        att = att + lax.dot_general(own(z), z, NT, preferred_element_type=f32) * mask
    st = st_ref[slot, head]
    lhs = jnp.concatenate([own(qb), att.astype(bf16)], axis=1)
    rhs_t = jnp.concatenate([st.astype(bf16), vt], axis=1)
    o = lax.dot_general(lhs, rhs_t, NT, preferred_element_type=f32)
    st_ref[slot, head] = st * decay + jnp.dot(vt, own(kd), preferred_element_type=f32)
    return o


def _mixer_body(*refs, nch, chunk_seq, tiles_per_seq, nprev):
    (x_ref, mod_ref, hg0_ref, gl0_ref, n1_ref, n2_ref, win_ref, lbl_ref, gup_ref, gbias_ref,
     hgn_ref, gln_ref, wout_ref, wrt_ref) = refs[:14]
    (x1_ref, hp_ref, lgt_ref, hg_ref, gl_ref, h_s, z_s, o_s, hf_s, slab_s, hgt_s, glt_s) = refs[14 + nprev:]
    i = pl.program_id(0)
    d = x_ref.shape[1]
    hdk, hdv = hg0_ref.shape[2], hg0_ref.shape[3]
    gdk, gdv = gl0_ref.shape[2], gl0_ref.shape[3]
    assert hdk == LANE and LANE % gdk == 0
    hgw = HG_HEADS * hdk
    gkw = GLA_HEADS * gdk
    gvw = GLA_HEADS * gdv
    nslot = hgt_s.shape[0]
    gla_lanes = lambda h: pl.ds((h * gdk) % LANE, gdk)

    def load_state():
        for s in range(nslot):
            for h in range(HG_HEADS):
                hgt_s[s, h] = hg0_ref[s, h].T
            for h in range(GLA_HEADS):
                glt_s[s, h] = jnp.zeros(glt_s.shape[2:], f32)
                glt_s[s, h, :, gla_lanes(h)] = gl0_ref[s, h].T

    if chunk_seq:
        load_state()
    else:
        pl.when(i % tiles_per_seq == 0)(load_state)

    lbl = lbl_ref[...]
    lbe = jnp.exp(lbl - jnp.max(lbl, axis=0, keepdims=True))
    lb = lbe[0:1, :] / jnp.sum(lbe, axis=0, keepdims=True)

    for j in range(nch):
        rows = pl.ds(j * CHUNK, CHUNK)
        m = mod_ref[j if chunk_seq else 0]
        xn = _rms(x_ref[rows, :]) * n1_ref[...]
        h_s[rows, :] = (xn * (1.0 + m[1:2, :]) + m[0:1, :]).astype(bf16)

    nz = z_s.shape[1]
    step = 512
    for n0 in range(0, nz, step):
        n1 = min(n0 + step, nz)
        z_s[:, n0:n1] = jnp.dot(h_s[...], win_ref[:, n0:n1], preferred_element_type=f32)

    o_hq, o_hf, o_hi, o_hgate = 0, hgw, 2 * hgw, 3 * hgw
    o_gq = 4 * hgw
    o_gk = o_gq + gkw
    o_gv = o_gk + gkw
    o_ggate = o_gv + gvw
    o_glr = o_ggate + gvw

    lvl = _pair_masks(CHUNK)
    lane = lax.broadcasted_iota(i32, (CHUNK, LANE), 1)
    heads_per_block = LANE // gdk
    for j in range(nch):
        rows = pl.ds(j * CHUNK, CHUNK)
        slot = j if chunk_seq else 0
        for h in range(HG_HEADS):
            lbh = lb[:, h * hdk:(h + 1) * hdk]
            hq = z_s[rows, pl.ds(o_hq + h * hdk, hdk)]
            sig, nsig = _sigmoid_pair(z_s[rows, pl.ds(o_hf + h * hdk, hdk)])
            g = jnp.log(lbh + (1.0 - lbh) * sig)
            k = (1.0 - lbh) * nsig
            q = hq * jax.nn.sigmoid(hq) * (hdk ** -0.5)
            zs, qb, kd, b_last = _chunk_scan(q, k, g)
            v = z_s[rows, pl.ds(o_hi + h * hdv, hdv)].T.astype(bf16)
            o = _chunk_head(zs, q.astype(bf16), k.astype(bf16), qb, kd, jnp.exp(b_last), v,
                            hgt_s, slot, h, None, lvl)
            hgate = z_s[rows, pl.ds(o_hgate + h * hdv, hdv)]
            o = _rms(o) * hgn_ref[:, pl.ds(h * hdv, hdv)] * jax.nn.sigmoid(hgate)
            o_s[rows, pl.ds(h * hdv, hdv)] = o.astype(bf16)
        u = _dot_split(z_s[rows, pl.ds(o_glr, LANE)], gup_ref[0], gup_ref[1], (((1,), (0,)), ((), ())))
        u = u + gbias_ref[...]
        loga = (jnp.minimum(u, 0.0) - jnp.log1p(jnp.exp(-jnp.abs(u)))) * (1.0 / GLA_GATE_NORMALIZER)
        for blk in range(GLA_HEADS // heads_per_block):
            q = z_s[rows, pl.ds(o_gq + blk * LANE, LANE)] * (gdk ** -0.5)
            k = z_s[rows, pl.ds(o_gk + blk * LANE, LANE)]
            zs, qb, kd, b_last = _chunk_scan(q, k, loga[:, blk * LANE:(blk + 1) * LANE])
            qd, kd0, decay = q.astype(bf16), k.astype(bf16), jnp.exp(b_last)
            for h in range(blk * heads_per_block, (blk + 1) * heads_per_block):
                lo = (h * gdk) % LANE
                keep = ((lane >= lo) & (lane < lo + gdk)).astype(bf16)
                v = z_s[rows, pl.ds(o_gv + h * gdv, gdv)].T.astype(bf16)
                o = _chunk_head(zs, qd, kd0, qb, kd, decay, v, glt_s, slot, h, keep, lvl)
                ggate = z_s[rows, pl.ds(o_ggate + h * gdv, gdv)]
                o = _rms(o) * gln_ref[...] * (ggate * jax.nn.sigmoid(ggate))
                o_s[rows, pl.ds(hgw + h * gdv, gdv)] = o.astype(bf16)

    a = jnp.dot(o_s[...], wout_ref[...], preferred_element_type=f32)
    for j in range(nch):
        rows = pl.ds(j * CHUNK, CHUNK)
        m = mod_ref[j if chunk_seq else 0]
        x1 = x_ref[rows, :] + m[2:3, :] * a[j * CHUNK:(j + 1) * CHUNK, :]
        x1_ref[rows, :] = x1
        hf = _rms(x1) * n2_ref[...] * (1.0 + m[4:5, :]) + m[3:4, :]
        hf_s[rows, :] = hf
        _store_rows(slab_s, j * CHUNK, hf)
    hp_ref[...] = slab_s[...].astype(bf16)
    h_hi, h_lo = _split(hf_s[...])
    nt = functools.partial(lax.dot_general, dimension_numbers=NT, preferred_element_type=f32)
    lgt_ref[...] = nt(wrt_ref[0], h_hi) + (nt(wrt_ref[1], h_hi) + nt(wrt_ref[0], h_lo))

    def store_state():
        for s in range(nslot):
            for h in range(HG_HEADS):
                hg_ref[s, h] = hgt_s[s, h].T
            for h in range(GLA_HEADS):
                gl_ref[s, h] = glt_s[s, h, :, gla_lanes(h)].T

    if chunk_seq:
        store_state()
    else:
        pl.when(i % tiles_per_seq == tiles_per_seq - 1)(store_state)


def _mixer(x, mod, hg0, gl0, p, chunk_seq, t_total, tile_offset, prev=()):
    nseq, L, d = x.shape
    t = nseq * L
    tm = MIX_TILE
    nch = tm // CHUNK
    if chunk_seq:
        assert L == CHUNK and nseq % nch == 0
        spt, tiles_per_seq = nch, 1
        seq_of = lambda i: i
    else:
        assert L % tm == 0
        spt, tiles_per_seq = 1, L // tm
        seq_of = lambda i: i // tiles_per_seq
    ne = p["w_rt"].shape[1]
    nz = p["w_in"].shape[1]
    _, hh, hdk, hdv = hg0.shape
    _, gh, gdk, gdv = gl0.shape
    const = lambda shape: pl.BlockSpec(shape, lambda i: (0,) * len(shape))
    nprev = len(prev)
    n_in = 14
    body = functools.partial(_mixer_body, nch=nch, chunk_seq=chunk_seq, tiles_per_seq=tiles_per_seq,
                             nprev=nprev)
    return pl.pallas_call(
        body,
        grid=(t // tm,),
        in_specs=[pl.BlockSpec((tm, d), lambda i: (i, 0)),
                  pl.BlockSpec((spt, 6, d), lambda i: (seq_of(i), 0, 0)),
                  pl.BlockSpec((spt, hh, hdk, hdv), lambda i: (seq_of(i), 0, 0, 0)),
                  pl.BlockSpec((spt, gh, gdk, gdv), lambda i: (seq_of(i), 0, 0, 0)),
                  const((1, d)), const((1, d)), const((d, nz)), const(p["lb_logits"].shape),
                  const(p["gk_up"].shape), const(p["gk_bias"].shape), const(p["hg_norm"].shape),
                  const(p["gla_norm"].shape), const((d, d)), const((2, ne, d))]
                 + [pl.BlockSpec(memory_space=pl.ANY)] * nprev,
        out_specs=[pl.BlockSpec((tm, d), lambda i: (i + tile_offset, 0)),
                   pl.BlockSpec((tm * NSLAB, LANE), lambda i: (i + tile_offset, 0)),
                   pl.BlockSpec((ne, tm), lambda i: (0, i + tile_offset)),
                   pl.BlockSpec((spt, hh, hdk, hdv), lambda i: (seq_of(i), 0, 0, 0)),
                   pl.BlockSpec((spt, gh, gdk, gdv), lambda i: (seq_of(i), 0, 0, 0))],
        out_shape=[jax.ShapeDtypeStruct((t_total, d), f32),
                   jax.ShapeDtypeStruct((t_total * NSLAB, LANE), bf16),
                   jax.ShapeDtypeStruct((ne, t_total), f32),
                   jax.ShapeDtypeStruct(hg0.shape, f32),
                   jax.ShapeDtypeStruct(gl0.shape, f32)],
        input_output_aliases={n_in + j: j for j in range(nprev)},
        scratch_shapes=[pltpu.VMEM((tm, d), bf16),
                        pltpu.VMEM((tm, nz), f32),
                        pltpu.VMEM((tm, d), bf16),
                        pltpu.VMEM((tm, d), f32),
                        pltpu.VMEM((tm * NSLAB, LANE), f32),
                        pltpu.VMEM((spt, hh, hdv, hdk), f32),
                        pltpu.VMEM((spt, gh, gdv, LANE), f32)],
        compiler_params=_cparams("arbitrary"),
        name="mixer_sample" if chunk_seq else "mixer_prompt",
    )(x.reshape(t, d), mod, hg0, gl0, p["norm1"], p["norm2"], p["w_in"], p["lb_logits"], p["gk_up"],
      p["gk_bias"], p["hg_norm"], p["gla_norm"], p["w_out"], p["w_rt"], *prev)


def _route_body(lg_ref, bias_ref, eidx_ref, wsel_ref, rank_ref, cnt_ref, run_s, s_s, cand_s):
    i = pl.program_id(0)
    ne, tt = lg_ref.shape
    gsz = ne // N_GROUPS
    ninf = -jnp.inf

    @pl.when(i == 0)
    def _():
        run_s[...] = jnp.zeros_like(run_s)

    row = lax.broadcasted_iota(i32, (ne, tt), 0)
    rg = lax.broadcasted_iota(i32, (gsz, tt), 0)

    gs = []
    for g in range(N_GROUPS):
        grows = pl.ds(g * gsz, gsz)
        sg = jax.nn.sigmoid(lg_ref[grows, :])
        s_s[grows, :] = sg
        blk = sg + bias_ref[grows, :]
        m1 = jnp.max(blk, axis=0, keepdims=True)
        first = jnp.min(jnp.where(blk == m1, rg, gsz), axis=0, keepdims=True)
        m2 = jnp.max(jnp.where(rg == first, ninf, blk), axis=0, keepdims=True)
        gs.append(m1 + m2)

    picked = [jnp.zeros((1, tt), f32) for _ in range(N_GROUPS)]
    for _ in range(TOPK_GROUPS):
        cur = [jnp.where(picked[g] > 0.5, ninf, gs[g]) for g in range(N_GROUPS)]
        m = functools.reduce(jnp.maximum, cur)
        gi = jnp.full((1, tt), N_GROUPS, i32)
        for g in reversed(range(N_GROUPS)):
            gi = jnp.where(cur[g] == m, g, gi)
        picked = [jnp.where(gi == g, 1.0, picked[g]) for g in range(N_GROUPS)]

    for g in range(N_GROUPS):
        grows = pl.ds(g * gsz, gsz)
        allowed = jnp.broadcast_to(picked[g], (gsz, tt)) > 0.5
        cand_s[grows, :] = jnp.where(allowed, s_s[grows, :] + bias_ref[grows, :], ninf)
    cand = cand_s[...]
    s = s_s[...]

    eis, ws = [], []
    chosen = jnp.zeros((ne, tt), jnp.bool_)
    for _ in range(TOP_K):
        m = jnp.max(cand, axis=0, keepdims=True)
        ei = jnp.min(jnp.where(cand == m, row, ne), axis=0, keepdims=True)
        hit = row == ei
        ws.append(jnp.sum(jnp.where(hit, s, 0.0), axis=0, keepdims=True))
        cand = jnp.where(hit, ninf, cand)
        chosen = chosen | hit
        eis.append(ei)
    wsum = functools.reduce(jnp.add, ws)
    scale = ROUTE_SCALE / wsum

    onehot = chosen.astype(bf16)
    ti = lax.broadcasted_iota(i32, (tt, tt), 0)
    tj = lax.broadcasted_iota(i32, (tt, tt), 1)
    before = (ti < tj).astype(bf16)
    prior = jnp.dot(onehot, before, preferred_element_type=f32) + run_s[:, 0:1]
    ct = eidx_ref.shape[2]
    for k in range(TOP_K):
        hit = row == eis[k]
        rank = jnp.sum(jnp.where(hit, prior, 0.0), axis=0, keepdims=True).astype(i32)
        wsel_ref[k:k + 1, :] = ws[k] * scale
        for part in range(tt // ct):
            eidx_ref[part, k:k + 1, :] = eis[k][:, part * ct:(part + 1) * ct]
            rank_ref[part, k:k + 1, :] = rank[:, part * ct:(part + 1) * ct]
    run_s[...] = run_s[...] + jnp.sum(chosen.astype(f32), axis=1, keepdims=True)
    cnt_ref[...] = run_s[...].astype(i32)


def _route(lgt, b_router):
    ne, t = lgt.shape
    tt, ct = ROUTE_TILE, COMB_TILE
    tiled = pl.BlockSpec((tt // ct, TOP_K, ct), lambda i: (i, 0, 0))
    tiled_shape = jax.ShapeDtypeStruct((t // ct, TOP_K, ct), i32)
    return pl.pallas_call(
        _route_body,
        grid=(t // tt,),
        in_specs=[pl.BlockSpec((ne, tt), lambda i: (0, i)),
                  pl.BlockSpec((ne, 1), lambda i: (0, 0))],
        out_specs=[tiled,
                   pl.BlockSpec((TOP_K, tt), lambda i: (0, i)),
                   tiled,
                   pl.BlockSpec((ne, LANE), lambda i: (0, 0))],
        out_shape=[tiled_shape, jax.ShapeDtypeStruct((TOP_K, t), f32), tiled_shape,
                   jax.ShapeDtypeStruct((ne, LANE), i32)],
        scratch_shapes=[pltpu.VMEM((ne, LANE), f32), pltpu.VMEM((ne, tt), f32), pltpu.VMEM((ne, tt), f32)],
        compiler_params=_cparams("arbitrary"),
        name="route",
    )(lgt, b_router.reshape(ne, 1))


def _dest_body(eidx_ref, rank_ref, pstart_ref, dest_ref):
    ne = pstart_ref.shape[0]
    ct = eidx_ref.shape[2]
    row = lax.broadcasted_iota(i32, (ne, ct), 0)
    ps = pstart_ref[...]
    for k in range(TOP_K):
        hit = row == eidx_ref[0, k:k + 1, :]
        base = jnp.sum(jnp.where(hit, ps, 0.0), axis=0, keepdims=True)
        dest_ref[0, k:k + 1, :] = base.astype(i32) + rank_ref[0, k:k + 1, :]


def _dest(eidx3, rank3, pstarts):
    ntile, _, ct = eidx3.shape
    ne = pstarts.shape[0]
    tile = pl.BlockSpec((1, TOP_K, ct), lambda i: (i, 0, 0))
    return pl.pallas_call(
        _dest_body,
        grid=(ntile,),
        in_specs=[tile, tile, pl.BlockSpec((ne, 1), lambda i: (0, 0))],
        out_specs=tile,
        out_shape=jax.ShapeDtypeStruct(eidx3.shape, i32),
        compiler_params=_cparams("arbitrary"),
        name="dest",
    )(eidx3, rank3, pstarts.astype(f32).reshape(ne, 1))


def _dispatch_body(dest_ref, padstart_ref, hf_ref, xs_ref, zero_s, sem, zsem):
    i = pl.program_id(0)
    tt = hf_ref.shape[0] // NSLAB
    ct = dest_ref.shape[2]
    ne = padstart_ref.shape[0]
    nfill = zero_s.shape[0] // NSLAB

    @pl.when(i == 0)
    def _():
        zero_s[...] = jnp.zeros_like(zero_s)

        def fill(e, c):
            pltpu.make_async_copy(zero_s, _slab(xs_ref, padstart_ref[e], nfill), zsem).start()
            return c

        lax.fori_loop(0, ne, fill, 0)

        def drain(e, c):
            pltpu.make_async_copy(zero_s, _slab(xs_ref, 0, nfill), zsem).wait()
            return c

        lax.fori_loop(0, ne, drain, 0)

    for part in range(tt // ct):
        def send(t, c, part=part):
            for k in range(TOP_K):
                pltpu.make_async_copy(_slab(hf_ref, part * ct + t, 1),
                                      _slab(xs_ref, dest_ref[part, k, t], 1), sem).start()
            return c

        lax.fori_loop(0, ct, send, 0)
    for k in range(TOP_K):
        pltpu.make_async_copy(hf_ref, _slab(xs_ref, 0, tt), sem).wait()


def _dispatch(dest3, padstart, hfs, p_rows):
    t = hfs.shape[0] // NSLAB
    tt = ROUTE_TILE
    ct = dest3.shape[2]
    return pl.pallas_call(
        _dispatch_body,
        grid=(t // tt,),
        in_specs=[pl.BlockSpec((tt // ct, TOP_K, ct), lambda i: (i, 0, 0), memory_space=pltpu.SMEM),
                  pl.BlockSpec(memory_space=pltpu.SMEM),
                  pl.BlockSpec((tt * NSLAB, LANE), lambda i: (i, 0))],
        out_specs=pl.BlockSpec(memory_space=pl.ANY),
        out_shape=jax.ShapeDtypeStruct(((p_rows + MOE_BLOCK) * NSLAB, LANE), hfs.dtype),
        scratch_shapes=[pltpu.VMEM((MOE_BLOCK * NSLAB, LANE), hfs.dtype), pltpu.SemaphoreType.DMA,
                        pltpu.SemaphoreType.DMA],
        compiler_params=_cparams("arbitrary"),
        name="dispatch",
    )(dest3, padstart, hfs)


class _Parts:
    def __init__(self, parts):
        self.parts = parts

    def start(self):
        for c in self.parts:
            c.start()

    def wait(self):
        for c in self.parts:
            c.wait()


def _experts_body(bstart_ref, nblk_ref, nused_ref, xs_ref, wg_ref, wu_ref, wd_ref, eo_ref,
                  xbuf, obuf, stage_s, wgu_s, wd_s, isem, osem):
    e = pl.program_id(0)
    blk = stage_s.shape[0] // NSLAB
    f = wg_ref.shape[2]
    nused = nused_ref[0]

    sub = blk // DMA_PARTS

    def in_copy(g):
        slot = lax.rem(g, IN_SLOTS)
        return _Parts([pltpu.make_async_copy(_slab(xs_ref, g * blk + j * sub, sub),
                                             _slab(xbuf, slot * blk + j * sub, sub), isem.at[slot])
                       for j in range(DMA_PARTS)])

    def out_copy(g):
        slot = lax.rem(g, OUT_SLOTS)
        return _Parts([pltpu.make_async_copy(_slab(obuf, slot * blk + j * sub, sub),
                                             _slab(eo_ref, g * blk + j * sub, sub), osem.at[slot])
                       for j in range(DMA_PARTS)])

    @pl.when(e == 0)
    def _():
        for g in range(IN_SLOTS - 1):
            pl.when(g < nused)(lambda g=g: in_copy(jnp.int32(g)).start())

    @pl.when(nblk_ref[e] > 0)
    def _():
        wgu_s[:, 0:f] = wg_ref[0].astype(bf16)
        wgu_s[:, f:2 * f] = wu_ref[0].astype(bf16)
        wd_s[...] = wd_ref[0].astype(bf16)

        def block(g, c):
            in_copy(g).wait()

            @pl.when(g + IN_SLOTS - 1 < nused)
            def _():
                in_copy(g + IN_SLOTS - 1).start()

            islot = pl.multiple_of(lax.rem(g, IN_SLOTS) * (blk * NSLAB), blk * NSLAB)
            stage_s[...] = xbuf[pl.ds(islot, blk * NSLAB), :].astype(f32)
            x = _load_rows(stage_s, 0, blk).astype(bf16)
            gu = jnp.dot(x, wgu_s[...], preferred_element_type=f32)
            gate = gu[:, 0:f]
            h = (gate * jax.nn.sigmoid(gate) * gu[:, f:2 * f]).astype(bf16)
            o = jnp.dot(h, wd_s[...], preferred_element_type=f32)

            @pl.when(g >= OUT_SLOTS)
            def _():
                out_copy(g - OUT_SLOTS).wait()

            _store_rows(obuf, lax.rem(g, OUT_SLOTS) * blk, o)
            out_copy(g).start()
            return c

        lax.fori_loop(bstart_ref[e], bstart_ref[e] + nblk_ref[e], block, 0)

    @pl.when(e == pl.num_programs(0) - 1)
    def _():
        for back in range(1, OUT_SLOTS + 1):
            pl.when(nused >= back)(lambda back=back: out_copy(nused - back).wait())


def _experts(bstart, nblk, nused, xs, w_gate, w_up, w_down, nb):
    ne, d, f = w_gate.shape
    assert d == NSLAB * LANE
    blk = MOE_BLOCK
    w_map = lambda e, *_: (e, 0, 0)
    grid_spec = pltpu.PrefetchScalarGridSpec(
        num_scalar_prefetch=3,
        grid=(ne,),
        in_specs=[pl.BlockSpec(memory_space=pl.ANY),
                  pl.BlockSpec((1, d, f), w_map),
                  pl.BlockSpec((1, d, f), w_map),
                  pl.BlockSpec((1, f, d), w_map)],
        out_specs=pl.BlockSpec(memory_space=pl.ANY),
        scratch_shapes=[pltpu.VMEM((IN_SLOTS * blk * NSLAB, LANE), xs.dtype),
                        pltpu.VMEM((OUT_SLOTS * blk * NSLAB, LANE), f32),
                        pltpu.VMEM((blk * NSLAB, LANE), f32),
                        pltpu.VMEM((d, 2 * f), bf16), pltpu.VMEM((f, d), bf16),
                        pltpu.SemaphoreType.DMA((IN_SLOTS,)), pltpu.SemaphoreType.DMA((OUT_SLOTS,))],
    )
    return pl.pallas_call(
        _experts_body,
        grid_spec=grid_spec,
        out_shape=jax.ShapeDtypeStruct((nb * blk * NSLAB, LANE), f32),
        compiler_params=_cparams("arbitrary"),
        name="experts",
    )(bstart, nblk, nused, xs, w_gate, w_up, w_down)


def _combine_body(dest_ref, destn_ref, eo_ref, x1_ref, hp_ref, w_ref, mod_ref, wsg_ref, wsu_ref, wsd_ref,
                  nf_ref, y_ref, buf, stage_s, sem, *, chunk_seq):
    i = pl.program_id(0)
    tt, d = x1_ref.shape
    slot = i & 1

    def gather(idx_ref, into):
        def fetch(t, c):
            for k in range(TOP_K):
                pltpu.make_async_copy(_slab(eo_ref, idx_ref[0, k, t], 1),
                                      _slab(buf, (into * TOP_K + k) * tt + t, 1), sem.at[into]).start()
            return c

        lax.fori_loop(0, tt, fetch, 0)

    pl.when(i == 0)(lambda: gather(dest_ref, slot))
    pl.when(i + 1 < pl.num_programs(0))(lambda: gather(destn_ref, 1 - slot))

    stage_s[...] = hp_ref[...].astype(f32)
    hf = _load_rows(stage_s, 0, tt).astype(bf16)
    g = jnp.dot(hf, wsg_ref[...], preferred_element_type=f32)
    u = jnp.dot(hf, wsu_ref[...], preferred_element_type=f32)
    acc = jnp.dot((g * jax.nn.sigmoid(g) * u).astype(bf16), wsd_ref[...], preferred_element_type=f32)

    for k in range(TOP_K):
        pltpu.make_async_copy(_slab(eo_ref, 0, tt), _slab(buf, (slot * TOP_K + k) * tt, tt), sem.at[slot]).wait()
    for k in range(TOP_K):
        acc = acc + _load_rows(buf, (slot * TOP_K + k) * tt, tt) * w_ref[:, k:k + 1]

    for j in range(tt // CHUNK):
        rows = pl.ds(j * CHUNK, CHUNK)
        g2 = mod_ref[j if chunk_seq else 0][5:6, :]
        y = x1_ref[rows, :] + g2 * acc[j * CHUNK:(j + 1) * CHUNK, :]
        y_ref[rows, :] = _rms(y) * nf_ref[...]


def _combine(dest3, eo, x1, hp, wsel, mod, p, nseq, seq_len, tile_offset):
    d = x1.shape[1]
    tt = COMB_TILE
    f = p["ws_gate"].shape[1]
    t = nseq * seq_len
    chunk_seq = seq_len == CHUNK
    if chunk_seq:
        spt, tiles_per_seq = tt // CHUNK, 1
        seq_of = lambda i: i
    else:
        assert seq_len % tt == 0
        spt, tiles_per_seq = 1, seq_len // tt
        seq_of = lambda i: i // tiles_per_seq
    body = functools.partial(_combine_body, chunk_seq=chunk_seq)
    const = lambda shape: pl.BlockSpec(shape, lambda i: (0,) * len(shape))
    tok = lambda w: pl.BlockSpec((tt, w), lambda i: (i + tile_offset, 0))
    ntiles = t // tt
    cur = pl.BlockSpec((1, TOP_K, tt), lambda i: (i + tile_offset, 0, 0), memory_space=pltpu.SMEM)
    nxt = pl.BlockSpec((1, TOP_K, tt), lambda i: (jnp.minimum(i + 1, ntiles - 1) + tile_offset, 0, 0),
                       memory_space=pltpu.SMEM)
    return pl.pallas_call(
        body,
        grid=(ntiles,),
        in_specs=[cur, nxt,
                  pl.BlockSpec(memory_space=pl.ANY),
                  tok(d), pl.BlockSpec((tt * NSLAB, LANE), lambda i: (i + tile_offset, 0)), tok(TOP_K),
                  pl.BlockSpec((spt, 6, d), lambda i: (seq_of(i), 0, 0)),
                  const((d, f)), const((d, f)), const((f, d)), const((1, d))],
        out_specs=pl.BlockSpec((tt, d), lambda i: (i, 0)),
        out_shape=jax.ShapeDtypeStruct((t, d), f32),
        scratch_shapes=[pltpu.VMEM((2 * TOP_K * tt * NSLAB, LANE), eo.dtype),
                        pltpu.VMEM((tt * NSLAB, LANE), f32), pltpu.SemaphoreType.DMA((2,))],
        compiler_params=_cparams("arbitrary"),
        name="combine_sample" if chunk_seq else "combine_prompt",
    )(dest3, dest3, eo, x1, hp, wsel, mod, p["ws_gate"], p["ws_up"], p["ws_down"], p["norm_final"])


def kernel(x_prompt, x_sample, state_hgrn, state_gla, c_prompt, c_sample, w_ada, b_ada, norm1, norm2, w_in,
           hg_lb_logits, gla_gk_up, gla_gk_bias, hg_out_norm, gla_out_norm, w_out, w_router, b_router,
           w_gate, w_up, w_down, ws_gate, ws_up, ws_down, norm_final):
    nbp, lp, d = x_prompt.shape
    nbs, ls, _ = x_sample.shape
    depth = w_in.shape[0]
    assert depth == 1
    ne = w_router.shape[2]
    hgw = hg_out_norm.shape[1]
    gkw = gla_gk_up.shape[2]
    rank = gla_gk_up.shape[1]
    d_in = w_in.shape[2]

    nz = d_in - rank + LANE
    w_in_p = jnp.pad(w_in[0].astype(bf16), ((0, 0), (0, nz - d_in)))
    p = dict(
        norm1=norm1[0].reshape(1, d), norm2=norm2[0].reshape(1, d), w_in=w_in_p,
        lb_logits=hg_lb_logits,
        gk_up=jnp.stack(_split(jnp.pad(gla_gk_up[0], ((0, LANE - rank), (0, 0))))),
        gk_bias=gla_gk_bias[0].reshape(1, gkw),
        hg_norm=hg_out_norm[0].reshape(1, hgw), gla_norm=gla_out_norm[0].reshape(1, -1),
        w_out=w_out[0].astype(bf16), w_rt=jnp.stack(_split(w_router[0].T)),
        ws_gate=ws_gate[0].astype(bf16), ws_up=ws_up[0].astype(bf16), ws_down=ws_down[0].astype(bf16),
        norm_final=norm_final.reshape(1, d),
    )

    c_all = jnp.concatenate([c_prompt, c_sample], axis=0)
    mod = _ada(c_all, w_ada[0], b_ada[0]).reshape(nbp + nbs, 6, d)

    zero_hg = jnp.zeros((nbp,) + state_hgrn.shape[2:], f32)
    zero_gl = jnp.zeros((nbp,) + state_gla.shape[2:], f32)
    tp, ts = nbp * lp, nbs * ls
    t = tp + ts
    x1, hp, lgt, hg_p, gl_p = _mixer(x_prompt, mod[:nbp], zero_hg, zero_gl, p, False, t, 0)
    x1, hp, lgt, hg_s, gl_s = _mixer(x_sample, mod[nbp:], state_hgrn[0], state_gla[0], p, True, t,
                                     tp // MIX_TILE, prev=(x1, hp, lgt))

    eidx3, wsel_t, rank3, cnt = _route(lgt, b_router[0])

    counts = cnt[:, 0]
    padded = (counts + MOE_BLOCK - 1) // MOE_BLOCK * MOE_BLOCK
    pends = jnp.cumsum(padded)
    pstarts = pends - padded
    nb = -(-(t * TOP_K + ne * (MOE_BLOCK - 1)) // MOE_BLOCK)
    nused = (pends[-1] // MOE_BLOCK).astype(i32).reshape(1)

    pstarts = pstarts.astype(i32)
    dest3 = _dest(eidx3, rank3, pstarts)
    xs = _dispatch(dest3, (pstarts + counts).astype(i32), hp, nb * MOE_BLOCK)
    eo = _experts(pstarts // MOE_BLOCK, (padded // MOE_BLOCK).astype(i32), nused, xs,
                  w_gate[0], w_up[0], w_down[0], nb)
    wsel = wsel_t.T
    y_prompt = _combine(dest3, eo, x1, hp, wsel, mod[:nbp], p, nbp, lp, 0)
    y_sample = _combine(dest3, eo, x1, hp, wsel, mod[nbp:], p, nbs, ls, tp // COMB_TILE)
    return (y_prompt.reshape(nbp, lp, d), y_sample.reshape(nbs, ls, d),
            hg_p[None], gl_p[None], hg_s[None], gl_s[None])
```
